```python
import jax, jax.numpy as jnp
from jax import lax
import numpy as np

D_MODEL = 1024
BATCH = 4
SEQ = 8192
DEPTH = 1

CTX_LEN = 256
GRID_W = 64
MIX_WIDTH = D_MODEL
FOURIER_WIDTH = MIX_WIDTH // 2
LRU_WIDTH = MIX_WIDTH // 2
FOURIER_HEADS = 8
FOURIER_HEAD_DIM = FOURIER_WIDTH // FOURIER_HEADS
LRU_HEADS = 8
LRU_HEAD_DIM = LRU_WIDTH // LRU_HEADS
IN_WIDTH = FOURIER_WIDTH + 2 * LRU_WIDTH
CONV_W = 4
LRU_C = 8.0
PEER_HEADS = 8
PEER_NKEYS = 128
PEER_EXPERTS = PEER_NKEYS * PEER_NKEYS
PEER_KEY_HALF = 128
PEER_TOPK = 16
PEER_CHUNK = 128
N_MOD = 6
EPS = 1e-6

kernel_name = 'hybrid_fnet_rglru_peer_block'


def rmsnorm(x, g):
    xf = x.astype(jnp.float32)
    y = xf * lax.rsqrt(jnp.mean(xf * xf, axis=-1, keepdims=True) + EPS)
    return (y * g.astype(jnp.float32)).astype(x.dtype)


def modulate(h, shift, scale):
    return h * (1 + scale) + shift


def dwconv(x, w, b):
    pad_l = CONV_W // 2
    pad_r = CONV_W - 1 - pad_l
    L = x.shape[-2]
    xp = jnp.pad(x, [(0, 0)] * (x.ndim - 2) + [(pad_l, pad_r), (0, 0)])
    y = b
    for k in range(CONV_W):
        y = y + xp[..., k:k + L, :] * w[k]
    return y


def fourier_mix(f):
    b_, L, _ = f.shape
    fh = f.astype(jnp.float32).reshape(b_, L, FOURIER_HEADS, FOURIER_HEAD_DIM)
    y = jnp.fft.fftn(fh, axes=(1, 3), norm='ortho').real
    return y.reshape(b_, L, FOURIER_WIDTH).astype(f.dtype)


def _lin_combine(left, right):
    a_l, b_l = left
    a_r, b_r = right
    return a_l * a_r, a_r * b_l + b_r


def rglru_scan(xc, w_a, b_a, w_x, b_x, lam, h0, reverse):
    b_, L, C = xc.shape
    xf = xc.astype(jnp.float32)
    xh = xf.reshape(b_, L, LRU_HEADS, LRU_HEAD_DIM)
    r = jax.nn.sigmoid(jnp.einsum('blhi,hij->blhj', xh, w_a.astype(jnp.float32)).reshape(b_, L, C) + b_a.astype(jnp.float32))
    i = jax.nn.sigmoid(jnp.einsum('blhi,hij->blhj', xh, w_x.astype(jnp.float32)).reshape(b_, L, C) + b_x.astype(jnp.float32))
    log_a = -LRU_C * r * jax.nn.softplus(-lam.astype(jnp.float32))
    a = jnp.exp(log_a)
    inp = jnp.sqrt(-jnp.expm1(2.0 * log_a)) * (i * xf)
    A, Bc = lax.associative_scan(_lin_combine, (a, inp), reverse=reverse, axis=1)
    h = A * h0[:, None, :] + Bc
    final = h[:, 0] if reverse else h[:, -1]
    return h, final


def peer_ffn(h, w_q, sub_keys, u_tab, v_tab):
    shp = h.shape
    flat = h.reshape(-1, PEER_CHUNK, shp[-1])

    def body(xc):
        T = xc.shape[0]
        q = (xc @ w_q).reshape(T, PEER_HEADS, 2, PEER_KEY_HALF)
        s = jnp.einsum('thpd,hpkd->thpk', q, sub_keys).astype(jnp.float32)
        s1, i1 = lax.top_k(s[:, :, 0], PEER_TOPK)
        s2, i2 = lax.top_k(s[:, :, 1], PEER_TOPK)
        cand = (s1[..., :, None] + s2[..., None, :]).reshape(T, PEER_HEADS, PEER_TOPK * PEER_TOPK)
        cidx = (i1[..., :, None] * PEER_NKEYS + i2[..., None, :]).reshape(T, PEER_HEADS, PEER_TOPK * PEER_TOPK)
        sc, pos = lax.top_k(cand, PEER_TOPK)
        idx = jnp.take_along_axis(cidx, pos, axis=-1)
        g = jax.nn.softmax(sc, axis=-1)
        u = u_tab[idx]
        act = jax.nn.gelu(jnp.einsum('thkd,td->thk', u, xc).astype(jnp.float32))
        return jnp.einsum('thk,thkd->td', (g * act).astype(xc.dtype), v_tab[idx])

    return lax.map(body, flat).reshape(shp)


def setup_inputs(seed: int = 0) -> dict:
    key = jax.random.key(seed)
    ks = jax.random.split(key, 24)
    D = D_MODEL

    def nrm(k, shape, scale):
        return jax.random.normal(k, shape, jnp.float32) * scale

    u = jax.random.uniform(ks[14], (DEPTH, 2, LRU_WIDTH), jnp.float32, minval=0.9, maxval=0.999)
    a0 = u ** (1.0 / LRU_C)
    return {
        'x': nrm(ks[0], (BATCH, SEQ, D), 1.0),
        'c': nrm(ks[1], (BATCH, D), 1.0),
        'ctx': nrm(ks[2], (BATCH, CTX_LEN, D), 1.0),
        'c_ctx': nrm(ks[3], (D,), 1.0),
        'w_mod': nrm(ks[4], (DEPTH, D, N_MOD * D), 0.5 * D ** -0.5),
        'b_mod': nrm(ks[5], (DEPTH, N_MOD * D), 0.02),
        'norm1_g': 1.0 + nrm(ks[6], (DEPTH, D), 0.02),
        'w_in': nrm(ks[7], (DEPTH, D, IN_WIDTH), D ** -0.5),
        'conv_w': nrm(ks[8], (DEPTH, CONV_W, LRU_WIDTH), CONV_W ** -0.5),
        'conv_b': nrm(ks[9], (DEPTH, LRU_WIDTH), 0.02),
        'lru_w_a': nrm(ks[10], (DEPTH, 2, LRU_HEADS, LRU_HEAD_DIM, LRU_HEAD_DIM), LRU_HEAD_DIM ** -0.5),
        'lru_b_a': nrm(ks[11], (DEPTH, 2, LRU_WIDTH), 0.02),
        'lru_w_x': nrm(ks[12], (DEPTH, 2, LRU_HEADS, LRU_HEAD_DIM, LRU_HEAD_DIM), LRU_HEAD_DIM ** -0.5),
        'lru_b_x': nrm(ks[13], (DEPTH, 2, LRU_WIDTH), 0.02),
        'lru_lambda': jnp.log(a0) - jnp.log1p(-a0),
        'fourier_out_g': 1.0 + nrm(ks[15], (DEPTH, FOURIER_WIDTH), 0.02),
        'lru_out_g': 1.0 + nrm(ks[16], (DEPTH, LRU_WIDTH), 0.02),
        'w_out': nrm(ks[17], (DEPTH, MIX_WIDTH, D), MIX_WIDTH ** -0.5),
        'norm2_g': 1.0 + nrm(ks[18], (DEPTH, D), 0.02),
        'peer_w_q': nrm(ks[19], (DEPTH, D, PEER_HEADS * 2 * PEER_KEY_HALF), D ** -0.5),
        'peer_sub_keys': nrm(ks[20], (DEPTH, PEER_HEADS, 2, PEER_NKEYS, PEER_KEY_HALF), PEER_KEY_HALF ** -0.5),
        'peer_u': nrm(ks[21], (DEPTH, PEER_EXPERTS, D), D ** -0.5),
        'peer_v': nrm(ks[22], (DEPTH, PEER_EXPERTS, D), PEER_HEADS ** -0.5),
        'final_norm_g': 1.0 + nrm(ks[23], (D,), 0.02),
    }


def reference(x, c, ctx, c_ctx, w_mod, b_mod, norm1_g, w_in, conv_w, conv_b, lru_w_a, lru_b_a,
              lru_w_x, lru_b_x, lru_lambda, fourier_out_g, lru_out_g, w_out, norm2_g,
              peer_w_q, peer_sub_keys, peer_u, peer_v, final_norm_g):
    B_, L, _ = x.shape
    ROWS = L // GRID_W
    C_LEN = ctx.shape[1]
    split_pts = [FOURIER_WIDTH, FOURIER_WIDTH + LRU_WIDTH]
    for l in range(DEPTH):
        update_ctx = l + 1 < DEPTH
        mod_x = jax.nn.silu(c) @ w_mod[l] + b_mod[l]
        mod_c = jax.nn.silu(c_ctx) @ w_mod[l] + b_mod[l]
        sh1_x, sc1_x, g1_x, sh2_x, sc2_x, g2_x = [m[:, None, :] for m in jnp.split(mod_x, N_MOD, axis=-1)]
        sh1_c, sc1_c, g1_c, sh2_c, sc2_c, g2_c = jnp.split(mod_c, N_MOD, axis=-1)

        hx = modulate(rmsnorm(x, norm1_g[l]), sh1_x, sc1_x)
        hc = modulate(rmsnorm(ctx, norm1_g[l]), sh1_c, sc1_c)
        fx, ux, gx = jnp.split(hx @ w_in[l], split_pts, axis=-1)
        fc, uc, gc = jnp.split(hc @ w_in[l], split_pts, axis=-1)

        ux = dwconv(ux.reshape(B_, ROWS, GRID_W, LRU_WIDTH), conv_w[l], conv_b[l]).reshape(B_, L, LRU_WIDTH)
        uc = dwconv(uc, conv_w[l], conv_b[l])
        h_zero = jnp.zeros((B_, LRU_WIDTH), jnp.float32)
        lat_dirs = []
        ctx_dirs = []
        for d, rev in ((0, False), (1, True)):
            hcd, fin = rglru_scan(uc, lru_w_a[l, d], lru_b_a[l, d], lru_w_x[l, d], lru_b_x[l, d],
                                  lru_lambda[l, d], h_zero, rev)
            hxd, _ = rglru_scan(ux, lru_w_a[l, d], lru_b_a[l, d], lru_w_x[l, d], lru_b_x[l, d],
                                lru_lambda[l, d], fin, rev)
            lat_dirs.append(hxd)
            ctx_dirs.append(hcd)
        rx = ((lat_dirs[0] + lat_dirs[1]) * jax.nn.gelu(gx.astype(jnp.float32))).astype(x.dtype)

        mx = jnp.concatenate([rmsnorm(fourier_mix(fx), fourier_out_g[l]),
                              rmsnorm(rx, lru_out_g[l])], axis=-1) @ w_out[l]
        x = x + g1_x * mx

        h2x = modulate(rmsnorm(x, norm2_g[l]), sh2_x, sc2_x)
        x = x + g2_x * peer_ffn(h2x, peer_w_q[l], peer_sub_keys[l], peer_u[l], peer_v[l])

        if update_ctx:
            rc = ((ctx_dirs[0] + ctx_dirs[1]) * jax.nn.gelu(gc.astype(jnp.float32))).astype(ctx.dtype)
            mc = jnp.concatenate([rmsnorm(fourier_mix(fc), fourier_out_g[l]),
                                  rmsnorm(rc, lru_out_g[l])], axis=-1) @ w_out[l]
            ctx = ctx + g1_c * mc
            h2c = modulate(rmsnorm(ctx, norm2_g[l]), sh2_c, sc2_c)
            ctx = ctx + g2_c * peer_ffn(h2c, peer_w_q[l], peer_sub_keys[l], peer_u[l], peer_v[l]).reshape(B_, C_LEN, D_MODEL)
    return rmsnorm(x, final_norm_g)
```

```python
import functools

import jax
import jax.numpy as jnp
import numpy as np
from jax import lax
from jax.experimental import pallas as pl
from jax.experimental.pallas import tpu as pltpu

F32 = jnp.float32
BF16 = jnp.bfloat16

D_MODEL = 1024
GRID_W = 64
FOURIER_WIDTH = 512
FOURIER_HEADS = 8
FOURIER_HEAD_DIM = FOURIER_WIDTH // FOURIER_HEADS
LRU_WIDTH = 512
LRU_HEADS = 8
IN_WIDTH = FOURIER_WIDTH + 2 * LRU_WIDTH
CONV_W = 4
LRU_C = 8.0
PEER_HEADS = 8
PEER_NKEYS = 128
PEER_KEY_HALF = 128
PEER_TOPK = 16
N_MOD = 6
EPS = 1e-6

VMEM_LIMIT = 48 * 1024 * 1024


def _params(*sem):
    return pltpu.CompilerParams(dimension_semantics=sem, vmem_limit_bytes=VMEM_LIMIT)


def _split_bf16(a):
    hi = a.astype(BF16)
    lo = (a - hi.astype(F32)).astype(BF16)
    return hi, lo


def _dot(a, b):
    return jnp.dot(a, b, preferred_element_type=F32)


def _dot3(a, b_hi, b_lo):
    a_hi, a_lo = _split_bf16(a)
    return _dot(a_hi, b_hi) + (_dot(a_hi, b_lo) + _dot(a_lo, b_hi))


def _rmsnorm(x, g):
    return x * lax.rsqrt(jnp.mean(x * x, axis=-1, keepdims=True) + EPS) * g


def _mod_kernel(c_ref, w_ref, b_ref, o_ref):
    c = c_ref[...]
    s = c * jax.nn.sigmoid(c)
    o_ref[...] = jnp.dot(s, w_ref[...], preferred_element_type=F32,
                         precision=lax.Precision.HIGHEST) + b_ref[...]


def _mod(cvec, w_mod, b_mod):
    n = w_mod.shape[1]
    bn = 768
    return pl.pallas_call(
        _mod_kernel,
        grid=(n // bn,),
        in_specs=[pl.BlockSpec((8, D_MODEL), lambda j: (0, 0)),
                  pl.BlockSpec((D_MODEL, bn), lambda j: (0, j)),
                  pl.BlockSpec((1, bn), lambda j: (0, j))],
        out_specs=pl.BlockSpec((8, bn), lambda j: (0, j)),
        out_shape=jax.ShapeDtypeStruct((8, n), F32),
        compiler_params=_params("arbitrary"),
        name="mod",
    )(cvec, w_mod, b_mod.reshape(1, n))


def _inproj_kernel(x_ref, sh_ref, sc_ref, g_ref, w_ref, f_ref, u_ref, gg_ref):
    h = _rmsnorm(x_ref[0], g_ref[...]) * (1.0 + sc_ref[0]) + sh_ref[0]
    o = _dot(h.astype(BF16), w_ref[...])
    f_ref[0] = o[:, :FOURIER_WIDTH]
    u_ref[0] = o[:, FOURIER_WIDTH:FOURIER_WIDTH + LRU_WIDTH]
    gg_ref[0] = o[:, FOURIER_WIDTH + LRU_WIDTH:]


def _in_proj(x, sh, sc, g, w_bf16, tm):
    b, l, _ = x.shape
    row = pl.BlockSpec((1, 1, D_MODEL), lambda i, j: (i, 0, 0))
    out = pl.BlockSpec((1, tm, FOURIER_WIDTH), lambda i, j: (i, j, 0))
    shp = jax.ShapeDtypeStruct((b, l, FOURIER_WIDTH), F32)
    return pl.pallas_call(
        _inproj_kernel,
        grid=(b, l // tm),
        in_specs=[pl.BlockSpec((1, tm, D_MODEL), lambda i, j: (i, j, 0)), row, row,
                  pl.BlockSpec((1, D_MODEL), lambda i, j: (0, 0)),
                  pl.BlockSpec((D_MODEL, IN_WIDTH), lambda i, j: (0, 0))],
        out_specs=[out, out, out],
        out_shape=[shp, shp, shp],
        compiler_params=_params("parallel", "arbitrary"),
        name="in_proj",
    )(x, sh, sc, g, w_bf16)


def _conv_rows(x, w, b, row_len):
    t = x.shape[0]
    col = lax.broadcasted_iota(jnp.int32, x.shape, 0) % row_len
    y = b + x * w[CONV_W // 2:CONV_W // 2 + 1]
    for k in range(CONV_W):
        off = k - CONV_W // 2
        if off == 0:
            continue
        shifted = pltpu.roll(x, (-off) % t, axis=0)
        ok = (col + off >= 0) & (col + off < row_len)
        y = y + jnp.where(ok, shifted, 0.0) * w[k:k + 1]
    return y


def _lru_terms(xc, wa, ba, wx, bx, sp):
    xb = xc.astype(BF16)
    r = jax.nn.sigmoid(_dot(xb, wa) + ba)
    i = jax.nn.sigmoid(_dot(xb, wx) + bx)
    log_a = (-LRU_C) * r * sp
    a = jnp.exp(log_a)
    th = jnp.tanh(log_a)
    one_minus_a2 = (-2.0) * th / (1.0 - th)
    return a, jnp.sqrt(one_minus_a2) * (i * xc)


def _scan_block(a, bv, h0, reverse):
    t = a.shape[0]
    row = lax.broadcasted_iota(jnp.int32, a.shape, 0)
    d = 1
    while d < t:
        if reverse:
            ok = row < t - d
            shift = t - d
        else:
            ok = row >= d
            shift = d
        a_s = jnp.where(ok, pltpu.roll(a, shift, axis=0), 1.0)
        b_s = jnp.where(ok, pltpu.roll(bv, shift, axis=0), 0.0)
        bv = a * b_s + bv
        a = a * a_s
        d *= 2
    return a * h0 + bv


def _lru_kernel(uf_ref, ub_ref, cw_ref, cb_ref, wa_ref, ba_ref, wx_ref, bx_ref, lam_ref, h0_ref,
                hf_ref, hb_ref, fin_ref, carry, *, row_len):
    j = pl.program_id(1)

    @pl.when(j == 0)
    def _():
        carry[...] = h0_ref[0]

    cw = cw_ref[...]
    cb = cb_ref[...]
    t = uf_ref.shape[1]
    lam = lam_ref[...]
    sp = jax.nn.softplus(-lam)
    for d, (u_ref, o_ref) in enumerate(((uf_ref, hf_ref), (ub_ref, hb_ref))):
        xc = _conv_rows(u_ref[0], cw, cb, row_len)
        a, bv = _lru_terms(xc, wa_ref[d], ba_ref[d:d + 1], wx_ref[d], bx_ref[d:d + 1], sp[d:d + 1])
        h = _scan_block(a, bv, carry[d:d + 1], reverse=(d == 1))
        o_ref[0] = h
        last = h[0:1] if d == 1 else h[t - 1:t]
        carry[d:d + 1] = last
    fin_ref[0] = carry[...]


def _lru(u, conv_w, conv_b, wa_bd, ba, wx_bd, bx, lam, h0, row_len, t):
    b, l, c = u.shape
    nb = l // t
    kern = functools.partial(_lru_kernel, row_len=row_len)
    full = lambda shape: pl.BlockSpec(shape, lambda i, j: (0,) * len(shape))
    blk_f = pl.BlockSpec((1, t, c), lambda i, j: (i, j, 0))
    blk_b = pl.BlockSpec((1, t, c), lambda i, j: (i, nb - 1 - j, 0))
    st = pl.BlockSpec((1, 2, c), lambda i, j: (i, 0, 0))
    return pl.pallas_call(
        kern,
        grid=(b, nb),
        in_specs=[blk_f, blk_b, full((CONV_W, c)), full((1, c)), full((2, c, c)), full((2, c)),
                  full((2, c, c)), full((2, c)), full((2, c)), st],
        out_specs=[blk_f, blk_b, st],
        out_shape=[jax.ShapeDtypeStruct((b, l, c), F32), jax.ShapeDtypeStruct((b, l, c), F32),
                   jax.ShapeDtypeStruct((b, 2, c), F32)],
        scratch_shapes=[pltpu.VMEM((2, c), F32)],
        compiler_params=_params("parallel", "arbitrary"),
        name="lru",
    )(u, u, conv_w, conv_b.reshape(1, c), wa_bd, ba, wx_bd, bx, lam, h0)


def _dft_rows_kernel(f_ref, c_ref, s_ref, ar_ref, ai_ref):
    x = f_ref[0].astype(BF16)
    ar_ref[0] = _dot(c_ref[...], x)
    ai_ref[0] = -_dot(s_ref[...], x)


def _dft_cols_kernel(ar_ref, ai_ref, tr_ref, ti_ref, c_ref, s_ref, bc_ref, bs_ref, g_ref, o_ref):
    ar = ar_ref[0, 0]
    ai = ai_ref[0, 0]
    tr = tr_ref[0]
    ti = ti_ref[0]
    br = (ar * tr - ai * ti).astype(BF16)
    bi = (ar * ti + ai * tr).astype(BF16)
    c = c_ref[...]
    s = s_ref[...]
    zr = _dot(c, br) + _dot(s, bi)
    zi = _dot(c, bi) - _dot(s, br)
    y = _dot(zr.astype(BF16), bc_ref[...]) + _dot(zi.astype(BF16), bs_ref[...])
    o_ref[0] = _rmsnorm(y, g_ref[...]).astype(o_ref.dtype)


def _dft_mats(n):
    k = np.arange(n)
    ang = 2.0 * np.pi * ((k[:, None] * k[None, :]) % n) / n
    return np.cos(ang), np.sin(ang)


def _fourier(f, g):
    b, l, w = f.shape
    n2 = 128
    n1 = l // n2
    scale = 1.0 / np.sqrt(float(l) * FOURIER_HEAD_DIM)
    c1, s1 = _dft_mats(n1)
    c2, s2 = _dft_mats(n2)
    cc, sc = _dft_mats(FOURIER_HEAD_DIM)
    eye = np.eye(FOURIER_HEADS)
    bc = np.kron(eye, cc) * scale
    bs = np.kron(eye, sc) * scale
    ang = 2.0 * np.pi * ((np.arange(n1)[:, None] * np.arange(n2)[None, :]) % l) / l
    tr = jnp.asarray(np.cos(ang)[:, :, None], F32)
    ti = jnp.asarray(-np.sin(ang)[:, :, None], F32)
    as_bf = lambda m: jnp.asarray(m, BF16)

    cb = min(8192, n2 * w)
    f2 = f.reshape(b, n1, n2 * w)
    blk = pl.BlockSpec((1, n1, cb), lambda i, j: (i, 0, j))
    mat1 = pl.BlockSpec((n1, n1), lambda i, j: (0, 0))
    shp = jax.ShapeDtypeStruct((b, n1, n2 * w), F32)
    ar, ai = pl.pallas_call(
        _dft_rows_kernel,
        grid=(b, n2 * w // cb),
        in_specs=[blk, mat1, mat1],
        out_specs=[blk, blk],
        out_shape=[shp, shp],
        compiler_params=_params("parallel", "arbitrary"),
        name="dft_rows",
    )(f2, as_bf(c1), as_bf(s1))

    ar = ar.reshape(b, n1, n2, w)
    ai = ai.reshape(b, n1, n2, w)
    a_blk = pl.BlockSpec((1, 1, n2, w), lambda i, j: (i, j, 0, 0))
    t_blk = pl.BlockSpec((1, n2, 1), lambda i, j: (j, 0, 0))
    mat2 = pl.BlockSpec((n2, n2), lambda i, j: (0, 0))
    matw = pl.BlockSpec((w, w), lambda i, j: (0, 0))
    out = pl.pallas_call(
        _dft_cols_kernel,
        grid=(b, n1),
        in_specs=[a_blk, a_blk, t_blk, t_blk, mat2, mat2, matw, matw,
                  pl.BlockSpec((1, w), lambda i, j: (0, 0))],
        out_specs=pl.BlockSpec((1, n2, w), lambda i, j: (i, 0, j)),
        out_shape=jax.ShapeDtypeStruct((b, n2, n1 * w), BF16),
        compiler_params=_params("parallel", "arbitrary"),
        name="dft_cols",
    )(ar, ai, tr, ti, as_bf(c2), as_bf(s2), as_bf(bc), as_bf(bs), g.reshape(1, w))
    return out.reshape(b, l, w)


def _mixout_kernel(hf_ref, hb_ref, gg_ref, fm_ref, x_ref, g1_ref, sh2_ref, sc2_ref, lg_ref,
                   wo_f_ref, wo_r_ref, n2g_ref, wq_hi_ref, wq_lo_ref, x1_ref, h2_ref, q_ref):
    rx = (hf_ref[0] + hb_ref[0]) * jax.nn.gelu(gg_ref[0])
    rxn = _rmsnorm(rx, lg_ref[...])
    mx = _dot(fm_ref[0], wo_f_ref[...]) + _dot(rxn.astype(BF16), wo_r_ref[...])
    x1 = x_ref[0] + g1_ref[0] * mx
    x1_ref[0] = x1
    h2 = _rmsnorm(x1, n2g_ref[...]) * (1.0 + sc2_ref[0]) + sh2_ref[0]
    h2_ref[0] = h2
    q_ref[0] = _dot3(h2, wq_hi_ref[...], wq_lo_ref[...])


def _mix_out(hf, hb, gg, fm, x, g1, sh2, sc2, lru_g, wo_f, wo_r, n2g, wq_hi, wq_lo, tm):
    b, l, d = x.shape
    nq = wq_hi.shape[1]
    half = pl.BlockSpec((1, tm, LRU_WIDTH), lambda i, j: (i, j, 0))
    tok = pl.BlockSpec((1, tm, d), lambda i, j: (i, j, 0))
    row = pl.BlockSpec((1, 1, d), lambda i, j: (i, 0, 0))
    full = lambda shape: pl.BlockSpec(shape, lambda i, j: (0,) * len(shape))
    return pl.pallas_call(
        _mixout_kernel,
        grid=(b, l // tm),
        in_specs=[half, half, half, half, tok, row, row, row, full((1, LRU_WIDTH)),
                  full((FOURIER_WIDTH, d)), full((LRU_WIDTH, d)), full((1, d)),
                  full((d, nq)), full((d, nq))],
        out_specs=[tok, tok, pl.BlockSpec((1, tm, nq), lambda i, j: (i, j, 0))],
        out_shape=[jax.ShapeDtypeStruct((b, l, d), F32), jax.ShapeDtypeStruct((b, l, d), F32),
                   jax.ShapeDtypeStruct((b, l, nq), F32)],
        compiler_params=_params("parallel", "arbitrary"),
        name="mix_out",
    )(hf, hb, gg, fm, x, g1, sh2, sc2, lru_g, wo_f, wo_r, n2g, wq_hi, wq_lo)


def _top16(s, payload=None):
    n = s.shape[0]
    iota = lax.broadcasted_iota(jnp.int32, s.shape, 0)
    vals, picks = [], []
    for _ in range(PEER_TOPK):
        m = jnp.max(s, axis=0, keepdims=True)
        pos = jnp.min(jnp.where(s == m, iota, n), axis=0, keepdims=True)
        hit = iota == pos
        vals.append(m)
        if payload is None:
            picks.append(pos)
        else:
            picks.append(jnp.max(jnp.where(hit, payload, -1), axis=0, keepdims=True))
        s = jnp.where(hit, -jnp.inf, s)
    return jnp.concatenate(vals, axis=0), jnp.concatenate(picks, axis=0)


def _topk_kernel(q_ref, k_hi_ref, k_lo_ref, idx_ref, g_ref):
    nt = (((1,), (1,)), ((), ()))
    for h in range(PEER_HEADS):
        tops = []
        for p in range(2):
            hp = 2 * h + p
            qs = q_ref[:, hp * PEER_KEY_HALF:(hp + 1) * PEER_KEY_HALF]
            q_hi, q_lo = _split_bf16(qs)
            k_hi = k_hi_ref[hp]
            k_lo = k_lo_ref[hp]
            dg = lambda a, b: lax.dot_general(a, b, nt, preferred_element_type=F32)
            s = dg(k_hi, q_hi) + (dg(k_hi, q_lo) + dg(k_lo, q_hi))
            tops.append(_top16(s))
        (s1, i1), (s2, i2) = tops
        cand = jnp.concatenate([s1[i:i + 1] + s2 for i in range(PEER_TOPK)], axis=0)
        cidx = jnp.concatenate([i1[i:i + 1] * PEER_NKEYS + i2 for i in range(PEER_TOPK)], axis=0)
        sc, idx = _top16(cand, cidx)
        e = jnp.exp(sc - sc[0:1])
        g = e / jnp.sum(e, axis=0, keepdims=True)
        idx_ref[h * PEER_TOPK:(h + 1) * PEER_TOPK, :] = idx
        g_ref[h * PEER_TOPK:(h + 1) * PEER_TOPK, :] = g


def _peer_topk(q, k_hi, k_lo, tm):
    t, nq = q.shape
    nsel = PEER_HEADS * PEER_TOPK
    full = pl.BlockSpec(k_hi.shape, lambda i: (0, 0, 0))
    out = pl.BlockSpec((nsel, tm), lambda i: (0, i))
    return pl.pallas_call(
        _topk_kernel,
        grid=(t // tm,),
        in_specs=[pl.BlockSpec((tm, nq), lambda i: (i, 0)), full, full],
        out_specs=[out, out],
        out_shape=[jax.ShapeDtypeStruct((nsel, t), jnp.int32), jax.ShapeDtypeStruct((nsel, t), F32)],
        compiler_params=_params("parallel"),
        name="peer_topk",
    )(q, k_hi, k_lo)


PEER_TB = 8
PEER_ROWS = PEER_TB * PEER_HEADS * PEER_TOPK


def _peer_kernel(idx_cur, idx_nxt, uv_hbm, h2_ref, gt_ref, x1_ref, g2_ref, fg_ref, o_ref, buf, sem):
    i = pl.program_id(0)
    n = pl.num_programs(0)
    slot = i % 2
    nsel = PEER_HEADS * PEER_TOPK

    def row_copy(e, s, r):
        return pltpu.make_async_copy(uv_hbm.at[pl.ds(e, 1)], buf.at[s, pl.ds(r, 1)], sem.at[s])

    def issue(idx_ref, s):
        def body(r, carry):
            row_copy(idx_ref[0, r], s, r).start()
            return carry
        lax.fori_loop(0, PEER_ROWS, body, 0, unroll=8)

    @pl.when(i == 0)
    def _():
        issue(idx_cur, 0)

    @pl.when(i + 1 < n)
    def _():
        issue(idx_nxt, 1 - slot)

    pltpu.make_async_copy(uv_hbm.at[pl.ds(0, PEER_ROWS)], buf.at[slot], sem.at[slot]).wait()

    gcol = gt_ref[0]
    rows = []
    for t in range(PEER_TB):
        u = buf[slot, t * nsel:(t + 1) * nsel, 0:D_MODEL]
        v = buf[slot, t * nsel:(t + 1) * nsel, D_MODEL:2 * D_MODEL]
        xt = h2_ref[t:t + 1, :]
        act = jnp.sum(u * xt, axis=-1, keepdims=True)
        w = gcol[:, t:t + 1] * jax.nn.gelu(act)
        rows.append(jnp.sum(v * w, axis=0, keepdims=True))
    y = jnp.concatenate(rows, axis=0)
    x2 = x1_ref[...] + g2_ref[0] * y
    o_ref[...] = _rmsnorm(x2, fg_ref[...])


def _peer_ffn(idx, gate_t, uv, h2, x1, g2, fg, seq_len):
    t, d = h2.shape
    nb = t // PEER_TB
    nsel = PEER_HEADS * PEER_TOPK
    idx3 = idx.reshape(nb, 1, PEER_ROWS)
    smem = lambda f: pl.BlockSpec((None, 1, PEER_ROWS), f, memory_space=pltpu.SMEM)
    tok = pl.BlockSpec((PEER_TB, d), lambda i: (i, 0))
    return pl.pallas_call(
        _peer_kernel,
        grid=(nb,),
        in_specs=[smem(lambda i: (i, 0, 0)),
                  smem(lambda i: (jnp.minimum(i + 1, nb - 1), 0, 0)),
                  pl.BlockSpec(memory_space=pl.ANY),
                  tok,
                  pl.BlockSpec((1, nsel, PEER_TB), lambda i: (i, 0, 0)),
                  tok,
                  pl.BlockSpec((1, 1, d), lambda i: (i * PEER_TB // seq_len, 0, 0)),
                  pl.BlockSpec((1, d), lambda i: (0, 0))],
        out_specs=tok,
        out_shape=jax.ShapeDtypeStruct((t, d), F32),
        scratch_shapes=[pltpu.VMEM((2, PEER_ROWS, 2 * d), F32), pltpu.SemaphoreType.DMA((2,))],
        compiler_params=_params("arbitrary"),
        name="peer_ffn",
    )(idx3, idx3, uv, h2, gate_t, x1, g2, fg)


def _block_diag(w):
    h, dh, _ = w.shape
    eye = jnp.eye(h, dtype=w.dtype)
    return (eye[:, None, :, None] * w[:, :, None, :]).reshape(h * dh, h * dh)


def kernel(x, c, ctx, c_ctx, w_mod, b_mod, norm1_g, w_in, conv_w, conv_b, lru_w_a, lru_b_a, lru_w_x,
           lru_b_x, lru_lambda, fourier_out_g, lru_out_g, w_out, norm2_g, peer_w_q, peer_sub_keys,
           peer_u, peer_v, final_norm_g):
    b, l, d = x.shape
    depth = w_mod.shape[0]
    assert depth == 1, "context stream update between layers is not implemented"
    assert d == D_MODEL and l % 1024 == 0 and b <= 7
    lyr = 0
    c_len = ctx.shape[1]

    cvec = jnp.zeros((8, d), F32).at[:b].set(c).at[b].set(c_ctx)
    mod = _mod(cvec, w_mod[lyr], b_mod[lyr])
    mod_x = mod[:b].reshape(b, N_MOD, 1, d)
    sh1, sc1, g1, sh2, sc2, g2 = [mod_x[:, k] for k in range(N_MOD)]
    mod_c = jnp.broadcast_to(mod[b].reshape(1, N_MOD, 1, d), (b, N_MOD, 1, d))

    n1g = norm1_g[lyr].reshape(1, d)
    w_in_b = w_in[lyr].astype(BF16)
    fx, ux, gx = _in_proj(x, sh1, sc1, n1g, w_in_b, tm=512)
    _, uc, _ = _in_proj(ctx, mod_c[:, 0], mod_c[:, 1], n1g, w_in_b, tm=c_len)

    wa = jnp.stack([_block_diag(lru_w_a[lyr, k]) for k in range(2)]).astype(BF16)
    wx = jnp.stack([_block_diag(lru_w_x[lyr, k]) for k in range(2)]).astype(BF16)
    lru_args = (conv_w[lyr], conv_b[lyr], wa, lru_b_a[lyr], wx, lru_b_x[lyr], lru_lambda[lyr])
    h_zero = jnp.zeros((b, 2, LRU_WIDTH), F32)
    _, _, fin = _lru(uc, *lru_args, h_zero, row_len=c_len, t=c_len)
    hf, hb, _ = _lru(ux, *lru_args, fin, row_len=GRID_W, t=256)

    fm = _fourier(fx, fourier_out_g[lyr])

    w_out_b = w_out[lyr].astype(BF16)
    wq_hi, wq_lo = _split_bf16(peer_w_q[lyr])
    x1, h2, q = _mix_out(hf, hb, gx, fm, x, g1, sh2, sc2, lru_out_g[lyr].reshape(1, LRU_WIDTH),
                         w_out_b[:FOURIER_WIDTH], w_out_b[FOURIER_WIDTH:], norm2_g[lyr].reshape(1, d),
                         wq_hi, wq_lo, tm=512)

    t = b * l
    nsel = PEER_HEADS * PEER_TOPK
    keys = peer_sub_keys[lyr].reshape(2 * PEER_HEADS, PEER_NKEYS, PEER_KEY_HALF)
    k_hi, k_lo = _split_bf16(keys)
    idx_t, gate_t = _peer_topk(q.reshape(t, -1), k_hi, k_lo, tm=256)

    nb = t // PEER_TB
    idx = idx_t.T
    gate_blk = gate_t.reshape(nsel, nb, PEER_TB).transpose(1, 0, 2)
    uv = jnp.concatenate([peer_u[lyr], peer_v[lyr]], axis=1)
    out = _peer_ffn(idx, gate_blk, uv, h2.reshape(t, d), x1.reshape(t, d), g2,
                    final_norm_g.reshape(1, d), seq_len=l)
    return out.reshape(b, l, d)
```

```python
import functools

import jax
import jax.numpy as jnp
import numpy as np
from jax import lax
from jax.experimental import pallas as pl
from jax.experimental.pallas import tpu as pltpu

F32 = jnp.float32
BF16 = jnp.bfloat16

D_MODEL = 1024
GRID_W = 64
FOURIER_WIDTH = 512
FOURIER_HEADS = 8
FOURIER_HEAD_DIM = FOURIER_WIDTH // FOURIER_HEADS
LRU_WIDTH = 512
LRU_HEADS = 8
IN_WIDTH = FOURIER_WIDTH + 2 * LRU_WIDTH
CONV_W = 4
LRU_C = 8.0
PEER_HEADS = 8
PEER_NKEYS = 128
PEER_KEY_HALF = 128
PEER_TOPK = 16
N_MOD = 6
EPS = 1e-6

VMEM_LIMIT = 48 * 1024 * 1024


def _params(*sem):
    return pltpu.CompilerParams(dimension_semantics=sem, vmem_limit_bytes=VMEM_LIMIT)


def _split_bf16(a):
    hi = a.astype(BF16)
    lo = (a - hi.astype(F32)).astype(BF16)
    return hi, lo


def _dot(a, b):
    return jnp.dot(a, b, preferred_element_type=F32)


def _dot3(a, b_hi, b_lo):
    a_hi, a_lo = _split_bf16(a)
    return _dot(a_hi, b_hi) + (_dot(a_hi, b_lo) + _dot(a_lo, b_hi))


def _rmsnorm(x, g):
    return x * lax.rsqrt(jnp.mean(x * x, axis=-1, keepdims=True) + EPS) * g


def _mod_kernel(c_ref, w_ref, b_ref, o_ref):
    c = c_ref[...]
    s = c * jax.nn.sigmoid(c)
    o_ref[...] = jnp.dot(s, w_ref[...], preferred_element_type=F32,
                         precision=lax.Precision.HIGHEST) + b_ref[...]


def _mod(cvec, w_mod, b_mod):
    n = w_mod.shape[1]
    bn = 768
    return pl.pallas_call(
        _mod_kernel,
        grid=(n // bn,),
        in_specs=[pl.BlockSpec((8, D_MODEL), lambda j: (0, 0)),
                  pl.BlockSpec((D_MODEL, bn), lambda j: (0, j)),
                  pl.BlockSpec((1, bn), lambda j: (0, j))],
        out_specs=pl.BlockSpec((8, bn), lambda j: (0, j)),
        out_shape=jax.ShapeDtypeStruct((8, n), F32),
        compiler_params=_params("arbitrary"),
        name="mod",
    )(cvec, w_mod, b_mod.reshape(1, n))


def _inproj_kernel(x_ref, sh_ref, sc_ref, g_ref, w_ref, f_ref, u_ref, gg_ref):
    h = _rmsnorm(x_ref[0], g_ref[...]) * (1.0 + sc_ref[0]) + sh_ref[0]
    o = _dot(h.astype(BF16), w_ref[...])
    f_ref[0] = o[:, :FOURIER_WIDTH]
    u_ref[0] = o[:, FOURIER_WIDTH:FOURIER_WIDTH + LRU_WIDTH]
    gg_ref[0] = o[:, FOURIER_WIDTH + LRU_WIDTH:]


def _in_proj(x, sh, sc, g, w_bf16, tm):
    b, l, _ = x.shape
    row = pl.BlockSpec((1, 1, D_MODEL), lambda i, j: (i, 0, 0))
    out = pl.BlockSpec((1, tm, FOURIER_WIDTH), lambda i, j: (i, j, 0))
    shp = jax.ShapeDtypeStruct((b, l, FOURIER_WIDTH), F32)
    return pl.pallas_call(
        _inproj_kernel,
        grid=(b, l // tm),
        in_specs=[pl.BlockSpec((1, tm, D_MODEL), lambda i, j: (i, j, 0)), row, row,
                  pl.BlockSpec((1, D_MODEL), lambda i, j: (0, 0)),
                  pl.BlockSpec((D_MODEL, IN_WIDTH), lambda i, j: (0, 0))],
        out_specs=[out, out, out],
        out_shape=[shp, shp, shp],
        compiler_params=_params("parallel", "arbitrary"),
        name="in_proj",
    )(x, sh, sc, g, w_bf16)


def _conv_rows(x, w, b, row_len):
    t = x.shape[0]
    col = lax.broadcasted_iota(jnp.int32, x.shape, 0) % row_len
    y = b + x * w[CONV_W // 2:CONV_W // 2 + 1]
    for k in range(CONV_W):
        off = k - CONV_W // 2
        if off == 0:
            continue
        shifted = pltpu.roll(x, (-off) % t, axis=0)
        ok = (col + off >= 0) & (col + off < row_len)
        y = y + jnp.where(ok, shifted, 0.0) * w[k:k + 1]
    return y


def _lru_terms(xc, wa, ba, wx, bx, sp):
    xb = xc.astype(BF16)
    r = jax.nn.sigmoid(_dot(xb, wa) + ba)
    i = jax.nn.sigmoid(_dot(xb, wx) + bx)
    log_a = (-LRU_C) * r * sp
    a = jnp.exp(log_a)
    th = jnp.tanh(log_a)
    one_minus_a2 = (-2.0) * th / (1.0 - th)
    return a, jnp.sqrt(one_minus_a2) * (i * xc)


def _scan_block(a, bv, h0, reverse):
    t = a.shape[0]
    row = lax.broadcasted_iota(jnp.int32, a.shape, 0)
    d = 1
    while d < t:
        if reverse:
            ok = row < t - d
            shift = t - d
        else:
            ok = row >= d
            shift = d
        a_s = jnp.where(ok, pltpu.roll(a, shift, axis=0), 1.0)
        b_s = jnp.where(ok, pltpu.roll(bv, shift, axis=0), 0.0)
        bv = a * b_s + bv
        a = a * a_s
        d *= 2
    return a * h0 + bv


def _lru_kernel(uf_ref, ub_ref, cw_ref, cb_ref, wa_ref, ba_ref, wx_ref, bx_ref, lam_ref, h0_ref,
                hf_ref, hb_ref, fin_ref, carry, *, row_len):
    j = pl.program_id(1)

    @pl.when(j == 0)
    def _():
        carry[...] = h0_ref[0]

    cw = cw_ref[...]
    cb = cb_ref[...]
    t = uf_ref.shape[1]
    lam = lam_ref[...]
    sp = jax.nn.softplus(-lam)
    for d, (u_ref, o_ref) in enumerate(((uf_ref, hf_ref), (ub_ref, hb_ref))):
        xc = _conv_rows(u_ref[0], cw, cb, row_len)
        a, bv = _lru_terms(xc, wa_ref[d], ba_ref[d:d + 1], wx_ref[d], bx_ref[d:d + 1], sp[d:d + 1])
        h = _scan_block(a, bv, carry[d:d + 1], reverse=(d == 1))
        o_ref[0] = h
        last = h[0:1] if d == 1 else h[t - 1:t]
        carry[d:d + 1] = last
    fin_ref[0] = carry[...]


def _lru(u, conv_w, conv_b, wa_bd, ba, wx_bd, bx, lam, h0, row_len, t):
    b, l, c = u.shape
    nb = l // t
    kern = functools.partial(_lru_kernel, row_len=row_len)
    full = lambda shape: pl.BlockSpec(shape, lambda i, j: (0,) * len(shape))
    blk_f = pl.BlockSpec((1, t, c), lambda i, j: (i, j, 0))
    blk_b = pl.BlockSpec((1, t, c), lambda i, j: (i, nb - 1 - j, 0))
    st = pl.BlockSpec((1, 2, c), lambda i, j: (i, 0, 0))
    return pl.pallas_call(
        kern,
        grid=(b, nb),
        in_specs=[blk_f, blk_b, full((CONV_W, c)), full((1, c)), full((2, c, c)), full((2, c)),
                  full((2, c, c)), full((2, c)), full((2, c)), st],
        out_specs=[blk_f, blk_b, st],
        out_shape=[jax.ShapeDtypeStruct((b, l, c), F32), jax.ShapeDtypeStruct((b, l, c), F32),
                   jax.ShapeDtypeStruct((b, 2, c), F32)],
        scratch_shapes=[pltpu.VMEM((2, c), F32)],
        compiler_params=_params("parallel", "arbitrary"),
        name="lru",
    )(u, u, conv_w, conv_b.reshape(1, c), wa_bd, ba, wx_bd, bx, lam, h0)


def _dft_rows_kernel(f_ref, c_ref, s_ref, ar_ref, ai_ref):
    x = f_ref[0].astype(BF16)
    ar_ref[0] = _dot(c_ref[...], x)
    ai_ref[0] = -_dot(s_ref[...], x)


def _dft_cols_kernel(ar_ref, ai_ref, tr_ref, ti_ref, c_ref, s_ref, bc_ref, bs_ref, g_ref, o_ref):
    ar = ar_ref[0, 0]
    ai = ai_ref[0, 0]
    tr = tr_ref[0]
    ti = ti_ref[0]
    br = (ar * tr - ai * ti).astype(BF16)
    bi = (ar * ti + ai * tr).astype(BF16)
    c = c_ref[...]
    s = s_ref[...]
    zr = _dot(c, br) + _dot(s, bi)
    zi = _dot(c, bi) - _dot(s, br)
    y = _dot(zr.astype(BF16), bc_ref[...]) + _dot(zi.astype(BF16), bs_ref[...])
    o_ref[0] = _rmsnorm(y, g_ref[...]).astype(o_ref.dtype)


def _dft_mats(n):
    k = np.arange(n)
    ang = 2.0 * np.pi * ((k[:, None] * k[None, :]) % n) / n
    return np.cos(ang), np.sin(ang)


def _fourier(f, g):
    b, l, w = f.shape
    n2 = 128
    n1 = l // n2
    scale = 1.0 / np.sqrt(float(l) * FOURIER_HEAD_DIM)
    c1, s1 = _dft_mats(n1)
    c2, s2 = _dft_mats(n2)
    cc, sc = _dft_mats(FOURIER_HEAD_DIM)
    eye = np.eye(FOURIER_HEADS)
    bc = np.kron(eye, cc) * scale
    bs = np.kron(eye, sc) * scale
    ang = 2.0 * np.pi * ((np.arange(n1)[:, None] * np.arange(n2)[None, :]) % l) / l
    tr = jnp.asarray(np.cos(ang)[:, :, None], F32)
    ti = jnp.asarray(-np.sin(ang)[:, :, None], F32)
    as_bf = lambda m: jnp.asarray(m, BF16)

    cb = min(8192, n2 * w)
    f2 = f.reshape(b, n1, n2 * w)
    blk = pl.BlockSpec((1, n1, cb), lambda i, j: (i, 0, j))
    mat1 = pl.BlockSpec((n1, n1), lambda i, j: (0, 0))
    shp = jax.ShapeDtypeStruct((b, n1, n2 * w), F32)
    ar, ai = pl.pallas_call(
        _dft_rows_kernel,
        grid=(b, n2 * w // cb),
        in_specs=[blk, mat1, mat1],
        out_specs=[blk, blk],
        out_shape=[shp, shp],
        compiler_params=_params("parallel", "arbitrary"),
        name="dft_rows",
    )(f2, as_bf(c1), as_bf(s1))

    ar = ar.reshape(b, n1, n2, w)
    ai = ai.reshape(b, n1, n2, w)
    a_blk = pl.BlockSpec((1, 1, n2, w), lambda i, j: (i, j, 0, 0))
    t_blk = pl.BlockSpec((1, n2, 1), lambda i, j: (j, 0, 0))
    mat2 = pl.BlockSpec((n2, n2), lambda i, j: (0, 0))
    matw = pl.BlockSpec((w, w), lambda i, j: (0, 0))
    out = pl.pallas_call(
        _dft_cols_kernel,
        grid=(b, n1),
        in_specs=[a_blk, a_blk, t_blk, t_blk, mat2, mat2, matw, matw,
                  pl.BlockSpec((1, w), lambda i, j: (0, 0))],
        out_specs=pl.BlockSpec((1, n2, w), lambda i, j: (i, 0, j)),
        out_shape=jax.ShapeDtypeStruct((b, n2, n1 * w), BF16),
        compiler_params=_params("parallel", "arbitrary"),
        name="dft_cols",
    )(ar, ai, tr, ti, as_bf(c2), as_bf(s2), as_bf(bc), as_bf(bs), g.reshape(1, w))
    return out.reshape(b, l, w)


def _mixout_kernel(hf_ref, hb_ref, gg_ref, fm_ref, x_ref, g1_ref, sh2_ref, sc2_ref, lg_ref,
                   wo_f_ref, wo_r_ref, n2g_ref, wq_hi_ref, wq_lo_ref, x1_ref, h2_ref, q_ref):
    rx = (hf_ref[0] + hb_ref[0]) * jax.nn.gelu(gg_ref[0])
    rxn = _rmsnorm(rx, lg_ref[...])
    mx = _dot(fm_ref[0], wo_f_ref[...]) + _dot(rxn.astype(BF16), wo_r_ref[...])
    x1 = x_ref[0] + g1_ref[0] * mx
    x1_ref[0] = x1
    h2 = _rmsnorm(x1, n2g_ref[...]) * (1.0 + sc2_ref[0]) + sh2_ref[0]
    h2_ref[0] = h2
    q_ref[0] = _dot3(h2, wq_hi_ref[...], wq_lo_ref[...])


def _mix_out(hf, hb, gg, fm, x, g1, sh2, sc2, lru_g, wo_f, wo_r, n2g, wq_hi, wq_lo, tm):
    b, l, d = x.shape
    nq = wq_hi.shape[1]
    half = pl.BlockSpec((1, tm, LRU_WIDTH), lambda i, j: (i, j, 0))
    tok = pl.BlockSpec((1, tm, d), lambda i, j: (i, j, 0))
    row = pl.BlockSpec((1, 1, d), lambda i, j: (i, 0, 0))
    full = lambda shape: pl.BlockSpec(shape, lambda i, j: (0,) * len(shape))
    return pl.pallas_call(
        _mixout_kernel,
        grid=(b, l // tm),
        in_specs=[half, half, half, half, tok, row, row, row, full((1, LRU_WIDTH)),
                  full((FOURIER_WIDTH, d)), full((LRU_WIDTH, d)), full((1, d)),
                  full((d, nq)), full((d, nq))],
        out_specs=[tok, tok, pl.BlockSpec((1, tm, nq), lambda i, j: (i, j, 0))],
        out_shape=[jax.ShapeDtypeStruct((b, l, d), F32), jax.ShapeDtypeStruct((b, l, d), F32),
                   jax.ShapeDtypeStruct((b, l, nq), F32)],
        compiler_params=_params("parallel", "arbitrary"),
        name="mix_out",
    )(hf, hb, gg, fm, x, g1, sh2, sc2, lru_g, wo_f, wo_r, n2g, wq_hi, wq_lo)


def _top16(s, payload=None):
    n = s.shape[0]
    iota = lax.broadcasted_iota(jnp.int32, s.shape, 0).astype(F32)
    vals, picks = [], []
    for _ in range(PEER_TOPK):
        m = jnp.max(s, axis=0, keepdims=True)
        pos = jnp.min(jnp.where(s == m, iota, float(n)), axis=0, keepdims=True)
        hit = iota == pos
        vals.append(m)
        if payload is None:
            picks.append(pos)
        else:
            picks.append(jnp.max(jnp.where(hit, payload, -1.0), axis=0, keepdims=True))
        s = jnp.where(hit, -jnp.inf, s)
    return jnp.concatenate(vals, axis=0), jnp.concatenate(picks, axis=0)


def _candidate_grid(s1, i1, s2, i2):
    k = PEER_TOPK
    row8 = lax.broadcasted_iota(jnp.int32, (8, s1.shape[1]), 0)
    vals = [s1[0:1] + s2, s1[1:2] + s2[0:8]]
    idxs = [i1[0:1] * PEER_NKEYS + i2, i1[1:2] * PEER_NKEYS + i2[0:8]]
    for i in range(2, 8):
        ok = row8 < k // (i + 1)
        vals.append(jnp.where(ok, s1[i:i + 1] + s2[0:8], -jnp.inf))
        idxs.append(i1[i:i + 1] * PEER_NKEYS + i2[0:8])
    vals.append(s1[8:16] + s2[0:1])
    idxs.append(i1[8:16] * PEER_NKEYS + i2[0:1])
    return jnp.concatenate(vals, axis=0), jnp.concatenate(idxs, axis=0)


def _topk_kernel(q_ref, k_hi_ref, k_lo_ref, idx_ref, g_ref):
    nt = (((1,), (1,)), ((), ()))
    for h in range(PEER_HEADS):
        tops = []
        for p in range(2):
            hp = 2 * h + p
            qs = q_ref[:, hp * PEER_KEY_HALF:(hp + 1) * PEER_KEY_HALF]
            q_hi, q_lo = _split_bf16(qs)
            k_hi = k_hi_ref[hp]
            k_lo = k_lo_ref[hp]
            dg = lambda a, b: lax.dot_general(a, b, nt, preferred_element_type=F32)
            s = dg(k_hi, q_hi) + (dg(k_hi, q_lo) + dg(k_lo, q_hi))
            tops.append(_top16(s))
        (s1, i1), (s2, i2) = tops
        cand, cidx = _candidate_grid(s1, i1, s2, i2)
        sc, idx = _top16(cand, cidx)
        e = jnp.exp(sc - sc[0:1])
        g = e / jnp.sum(e, axis=0, keepdims=True)
        idx_ref[h * PEER_TOPK:(h + 1) * PEER_TOPK, :] = idx.astype(jnp.int32)
        g_ref[h * PEER_TOPK:(h + 1) * PEER_TOPK, :] = g


def _peer_topk(q, k_hi, k_lo, tm):
    t, nq = q.shape
    nsel = PEER_HEADS * PEER_TOPK
    full = pl.BlockSpec(k_hi.shape, lambda i: (0, 0, 0))
    out = pl.BlockSpec((nsel, tm), lambda i: (0, i))
    return pl.pallas_call(
        _topk_kernel,
        grid=(t // tm,),
        in_specs=[pl.BlockSpec((tm, nq), lambda i: (i, 0)), full, full],
        out_specs=[out, out],
        out_shape=[jax.ShapeDtypeStruct((nsel, t), jnp.int32), jax.ShapeDtypeStruct((nsel, t), F32)],
        compiler_params=_params("parallel"),
        name="peer_topk",
    )(q, k_hi, k_lo)


PEER_TB = 8
PEER_ROWS = PEER_TB * PEER_HEADS * PEER_TOPK


def _peer_kernel(idx_first, idx_a, idx_b, uv_hbm, h2_ref, gt_ref, x1_ref, g2_ref, fg_ref, o_ref,
                 buf0, buf1, sem):
    j = pl.program_id(0)
    n = pl.num_programs(0)
    nsel = PEER_HEADS * PEER_TOPK
    bufs = (buf0, buf1)

    def row_copy(e, s, r):
        return pltpu.make_async_copy(uv_hbm.at[e], bufs[s].at[pl.ds(r, 1)], sem.at[s])

    def wait_slot(s):
        pltpu.make_async_copy(bufs[s], bufs[s], sem.at[s]).wait()

    @pl.when(j == 0)
    def _():
        def body(r, carry):
            row_copy(idx_first[0, r], 0, r).start()
            return carry
        lax.fori_loop(0, PEER_ROWS, body, 0, unroll=8)

    g2 = g2_ref[0]
    fg = fg_ref[...]
    for s, idx_next in ((0, idx_a), (1, idx_b)):
        wait_slot(s)
        for r in range(PEER_ROWS):
            row_copy(idx_next[0, r], 1 - s, r).start(priority=r % 2)
        gcol = gt_ref[s]
        rows = []
        for t in range(PEER_TB):
            u = bufs[s][t * nsel:(t + 1) * nsel, 0:D_MODEL]
            v = bufs[s][t * nsel:(t + 1) * nsel, D_MODEL:2 * D_MODEL]
            xt = h2_ref[s * PEER_TB + t:s * PEER_TB + t + 1, :]
            act = jnp.sum(u * xt, axis=-1, keepdims=True)
            w = gcol[:, t:t + 1] * jax.nn.gelu(act)
            rows.append(jnp.sum(v * w, axis=0, keepdims=True))
        y = jnp.concatenate(rows, axis=0)
        tok = slice(s * PEER_TB, (s + 1) * PEER_TB)
        o_ref[tok, :] = _rmsnorm(x1_ref[tok, :] + g2 * y, fg)

    @pl.when(j == n - 1)
    def _():
        wait_slot(0)


def _peer_ffn(idx, gate_t, uv, h2, x1, g2, fg, seq_len):
    t, d = h2.shape
    nb = t // PEER_TB
    nsel = PEER_HEADS * PEER_TOPK
    idx3 = idx.reshape(nb, 1, PEER_ROWS)
    smem = lambda f: pl.BlockSpec((None, 1, PEER_ROWS), f, memory_space=pltpu.SMEM)
    tok = pl.BlockSpec((2 * PEER_TB, d), lambda j: (j, 0))
    return pl.pallas_call(
        _peer_kernel,
        grid=(nb // 2,),
        in_specs=[smem(lambda j: (0, 0, 0)),
                  smem(lambda j: (2 * j + 1, 0, 0)),
                  smem(lambda j: (jnp.minimum(2 * j + 2, nb - 1), 0, 0)),
                  pl.BlockSpec(memory_space=pl.ANY),
                  tok,
                  pl.BlockSpec((2, nsel, PEER_TB), lambda j: (j, 0, 0)),
                  tok,
                  pl.BlockSpec((1, 1, d), lambda j: (j * 2 * PEER_TB // seq_len, 0, 0)),
                  pl.BlockSpec((1, d), lambda j: (0, 0))],
        out_specs=tok,
        out_shape=jax.ShapeDtypeStruct((t, d), F32),
        scratch_shapes=[pltpu.VMEM((PEER_ROWS, 2 * d), F32), pltpu.VMEM((PEER_ROWS, 2 * d), F32),
                        pltpu.SemaphoreType.DMA((2,))],
        compiler_params=_params("arbitrary"),
        name="peer_ffn",
    )(idx3, idx3, idx3, uv, h2, gate_t, x1, g2, fg)


def _block_diag(w):
    h, dh, _ = w.shape
    eye = jnp.eye(h, dtype=w.dtype)
    return (eye[:, None, :, None] * w[:, :, None, :]).reshape(h * dh, h * dh)


def kernel(x, c, ctx, c_ctx, w_mod, b_mod, norm1_g, w_in, conv_w, conv_b, lru_w_a, lru_b_a, lru_w_x,
           lru_b_x, lru_lambda, fourier_out_g, lru_out_g, w_out, norm2_g, peer_w_q, peer_sub_keys,
           peer_u, peer_v, final_norm_g):
    b, l, d = x.shape
    depth = w_mod.shape[0]
    assert depth == 1, "context stream update between layers is not implemented"
    assert d == D_MODEL and l % 1024 == 0 and b <= 7
    lyr = 0
    c_len = ctx.shape[1]

    cvec = jnp.zeros((8, d), F32).at[:b].set(c).at[b].set(c_ctx)
    mod = _mod(cvec, w_mod[lyr], b_mod[lyr])
    mod_x = mod[:b].reshape(b, N_MOD, 1, d)
    sh1, sc1, g1, sh2, sc2, g2 = [mod_x[:, k] for k in range(N_MOD)]
    mod_c = jnp.broadcast_to(mod[b].reshape(1, N_MOD, 1, d), (b, N_MOD, 1, d))

    n1g = norm1_g[lyr].reshape(1, d)
    w_in_b = w_in[lyr].astype(BF16)
    fx, ux, gx = _in_proj(x, sh1, sc1, n1g, w_in_b, tm=512)
    _, uc, _ = _in_proj(ctx, mod_c[:, 0], mod_c[:, 1], n1g, w_in_b, tm=c_len)

    wa = jnp.stack([_block_diag(lru_w_a[lyr, k]) for k in range(2)]).astype(BF16)
    wx = jnp.stack([_block_diag(lru_w_x[lyr, k]) for k in range(2)]).astype(BF16)
    lru_args = (conv_w[lyr], conv_b[lyr], wa, lru_b_a[lyr], wx, lru_b_x[lyr], lru_lambda[lyr])
    h_zero = jnp.zeros((b, 2, LRU_WIDTH), F32)
    _, _, fin = _lru(uc, *lru_args, h_zero, row_len=c_len, t=c_len)
    hf, hb, _ = _lru(ux, *lru_args, fin, row_len=GRID_W, t=256)

    fm = _fourier(fx, fourier_out_g[lyr])

    w_out_b = w_out[lyr].astype(BF16)
    wq_hi, wq_lo = _split_bf16(peer_w_q[lyr])
    x1, h2, q = _mix_out(hf, hb, gx, fm, x, g1, sh2, sc2, lru_out_g[lyr].reshape(1, LRU_WIDTH),
                         w_out_b[:FOURIER_WIDTH], w_out_b[FOURIER_WIDTH:], norm2_g[lyr].reshape(1, d),
                         wq_hi, wq_lo, tm=512)

    t = b * l
    nsel = PEER_HEADS * PEER_TOPK
    keys = peer_sub_keys[lyr].reshape(2 * PEER_HEADS, PEER_NKEYS, PEER_KEY_HALF)
    k_hi, k_lo = _split_bf16(keys)
    idx_t, gate_t = _peer_topk(q.reshape(t, -1), k_hi, k_lo, tm=256)

    nb = t // PEER_TB
    idx = idx_t.T
    gate_blk = gate_t.reshape(nsel, nb, PEER_TB).transpose(1, 0, 2)
    uv = jnp.concatenate([peer_u[lyr], peer_v[lyr]], axis=1).reshape(-1, 1, 2 * d)
    out = _peer_ffn(idx, gate_blk, uv, h2.reshape(t, d), x1.reshape(t, d), g2,
                    final_norm_g.reshape(1, d), seq_len=l)
    return out.reshape(b, l, d)
```

```python
import functools

import jax
import jax.numpy as jnp
import numpy as np
from jax import lax
from jax.experimental import pallas as pl
from jax.experimental.pallas import tpu as pltpu

F32 = jnp.float32
BF16 = jnp.bfloat16

D_MODEL = 1024
GRID_W = 64
FOURIER_WIDTH = 512
FOURIER_HEADS = 8
FOURIER_HEAD_DIM = FOURIER_WIDTH // FOURIER_HEADS
LRU_WIDTH = 512
LRU_HEADS = 8
IN_WIDTH = FOURIER_WIDTH + 2 * LRU_WIDTH
CONV_W = 4
LRU_C = 8.0
PEER_HEADS = 8
PEER_NKEYS = 128
PEER_KEY_HALF = 128
PEER_TOPK = 16
N_MOD = 6
EPS = 1e-6

VMEM_LIMIT = 48 * 1024 * 1024


def _params(*sem):
    return pltpu.CompilerParams(dimension_semantics=sem, vmem_limit_bytes=VMEM_LIMIT)


def _split_bf16(a):
    hi = a.astype(BF16)
    lo = (a - hi.astype(F32)).astype(BF16)
    return hi, lo


def _dot(a, b):
    return jnp.dot(a, b, preferred_element_type=F32)


def _dot3(a, b_hi, b_lo):
    a_hi, a_lo = _split_bf16(a)
    return _dot(a_hi, b_hi) + (_dot(a_hi, b_lo) + _dot(a_lo, b_hi))


def _rmsnorm(x, g):
    return x * lax.rsqrt(jnp.mean(x * x, axis=-1, keepdims=True) + EPS) * g


def _mod_kernel(c_ref, w_ref, b_ref, o_ref):
    c = c_ref[...]
    s = c * jax.nn.sigmoid(c)
    o_ref[...] = jnp.dot(s, w_ref[...], preferred_element_type=F32,
                         precision=lax.Precision.HIGHEST) + b_ref[...]


def _mod(cvec, w_mod, b_mod):
    n = w_mod.shape[1]
    bn = 768
    return pl.pallas_call(
        _mod_kernel,
        grid=(n // bn,),
        in_specs=[pl.BlockSpec((8, D_MODEL), lambda j: (0, 0)),
                  pl.BlockSpec((D_MODEL, bn), lambda j: (0, j)),
                  pl.BlockSpec((1, bn), lambda j: (0, j))],
        out_specs=pl.BlockSpec((8, bn), lambda j: (0, j)),
        out_shape=jax.ShapeDtypeStruct((8, n), F32),
        compiler_params=_params("arbitrary"),
        name="mod",
    )(cvec, w_mod, b_mod.reshape(1, n))


def _inproj_kernel(x_ref, sh_ref, sc_ref, g_ref, w_ref, f_ref, u_ref, gg_ref):
    h = _rmsnorm(x_ref[0], g_ref[...]) * (1.0 + sc_ref[0]) + sh_ref[0]
    o = _dot(h.astype(BF16), w_ref[...])
    f_ref[0] = o[:, :FOURIER_WIDTH]
    u_ref[0] = o[:, FOURIER_WIDTH:FOURIER_WIDTH + LRU_WIDTH]
    gg_ref[0] = o[:, FOURIER_WIDTH + LRU_WIDTH:]


def _in_proj(x, sh, sc, g, w_bf16, tm):
    b, l, _ = x.shape
    row = pl.BlockSpec((1, 1, D_MODEL), lambda i, j: (i, 0, 0))
    out = pl.BlockSpec((1, tm, FOURIER_WIDTH), lambda i, j: (i, j, 0))
    shp = jax.ShapeDtypeStruct((b, l, FOURIER_WIDTH), F32)
    return pl.pallas_call(
        _inproj_kernel,
        grid=(b, l // tm),
        in_specs=[pl.BlockSpec((1, tm, D_MODEL), lambda i, j: (i, j, 0)), row, row,
                  pl.BlockSpec((1, D_MODEL), lambda i, j: (0, 0)),
                  pl.BlockSpec((D_MODEL, IN_WIDTH), lambda i, j: (0, 0))],
        out_specs=[out, out, out],
        out_shape=[shp, shp, shp],
        compiler_params=_params("parallel", "arbitrary"),
        name="in_proj",
    )(x, sh, sc, g, w_bf16)


def _conv_rows(x, w, b, row_len):
    t = x.shape[0]
    col = lax.broadcasted_iota(jnp.int32, x.shape, 0) % row_len
    y = b + x * w[CONV_W // 2:CONV_W // 2 + 1]
    for k in range(CONV_W):
        off = k - CONV_W // 2
        if off == 0:
            continue
        shifted = pltpu.roll(x, (-off) % t, axis=0)
        ok = (col + off >= 0) & (col + off < row_len)
        y = y + jnp.where(ok, shifted, 0.0) * w[k:k + 1]
    return y


def _lru_terms(xc, wa, ba, wx, bx, sp):
    xb = xc.astype(BF16)
    r = jax.nn.sigmoid(_dot(xb, wa) + ba)
    i = jax.nn.sigmoid(_dot(xb, wx) + bx)
    log_a = (-LRU_C) * r * sp
    a = jnp.exp(log_a)
    th = jnp.tanh(log_a)
    one_minus_a2 = (-2.0) * th / (1.0 - th)
    return a, jnp.sqrt(one_minus_a2) * (i * xc)


def _scan_block(a, bv, h0, reverse):
    t = a.shape[0]
    row = lax.broadcasted_iota(jnp.int32, a.shape, 0)
    d = 1
    while d < t:
        if reverse:
            ok = row < t - d
            shift = t - d
        else:
            ok = row >= d
            shift = d
        a_s = jnp.where(ok, pltpu.roll(a, shift, axis=0), 1.0)
        b_s = jnp.where(ok, pltpu.roll(bv, shift, axis=0), 0.0)
        bv = a * b_s + bv
        a = a * a_s
        d *= 2
    return a * h0 + bv


def _lru_kernel(uf_ref, ub_ref, cw_ref, cb_ref, wa_ref, ba_ref, wx_ref, bx_ref, lam_ref, h0_ref,
                hf_ref, hb_ref, fin_ref, carry, *, row_len):
    j = pl.program_id(1)

    @pl.when(j == 0)
    def _():
        carry[...] = h0_ref[0]

    cw = cw_ref[...]
    cb = cb_ref[...]
    t = uf_ref.shape[1]
    lam = lam_ref[...]
    sp = jax.nn.softplus(-lam)
    for d, (u_ref, o_ref) in enumerate(((uf_ref, hf_ref), (ub_ref, hb_ref))):
        xc = _conv_rows(u_ref[0], cw, cb, row_len)
        a, bv = _lru_terms(xc, wa_ref[d], ba_ref[d:d + 1], wx_ref[d], bx_ref[d:d + 1], sp[d:d + 1])
        h = _scan_block(a, bv, carry[d:d + 1], reverse=(d == 1))
        o_ref[0] = h
        last = h[0:1] if d == 1 else h[t - 1:t]
        carry[d:d + 1] = last
    fin_ref[0] = carry[...]


def _lru(u, conv_w, conv_b, wa_bd, ba, wx_bd, bx, lam, h0, row_len, t):
    b, l, c = u.shape
    nb = l // t
    kern = functools.partial(_lru_kernel, row_len=row_len)
    full = lambda shape: pl.BlockSpec(shape, lambda i, j: (0,) * len(shape))
    blk_f = pl.BlockSpec((1, t, c), lambda i, j: (i, j, 0))
    blk_b = pl.BlockSpec((1, t, c), lambda i, j: (i, nb - 1 - j, 0))
    st = pl.BlockSpec((1, 2, c), lambda i, j: (i, 0, 0))
    return pl.pallas_call(
        kern,
        grid=(b, nb),
        in_specs=[blk_f, blk_b, full((CONV_W, c)), full((1, c)), full((2, c, c)), full((2, c)),
                  full((2, c, c)), full((2, c)), full((2, c)), st],
        out_specs=[blk_f, blk_b, st],
        out_shape=[jax.ShapeDtypeStruct((b, l, c), F32), jax.ShapeDtypeStruct((b, l, c), F32),
                   jax.ShapeDtypeStruct((b, 2, c), F32)],
        scratch_shapes=[pltpu.VMEM((2, c), F32)],
        compiler_params=_params("parallel", "arbitrary"),
        name="lru",
    )(u, u, conv_w, conv_b.reshape(1, c), wa_bd, ba, wx_bd, bx, lam, h0)


def _dft_rows_kernel(f_ref, c_ref, s_ref, ar_ref, ai_ref):
    x = f_ref[0].astype(BF16)
    ar_ref[0] = _dot(c_ref[...], x)
    ai_ref[0] = -_dot(s_ref[...], x)


def _dft_cols_kernel(ar_ref, ai_ref, tr_ref, ti_ref, c_ref, s_ref, bc_ref, bs_ref, g_ref, o_ref):
    ar = ar_ref[0, 0]
    ai = ai_ref[0, 0]
    tr = tr_ref[0]
    ti = ti_ref[0]
    br = (ar * tr - ai * ti).astype(BF16)
    bi = (ar * ti + ai * tr).astype(BF16)
    c = c_ref[...]
    s = s_ref[...]
    zr = _dot(c, br) + _dot(s, bi)
    zi = _dot(c, bi) - _dot(s, br)
    y = _dot(zr.astype(BF16), bc_ref[...]) + _dot(zi.astype(BF16), bs_ref[...])
    o_ref[0] = _rmsnorm(y, g_ref[...]).astype(o_ref.dtype)


def _dft_mats(n):
    k = np.arange(n)
    ang = 2.0 * np.pi * ((k[:, None] * k[None, :]) % n) / n
    return np.cos(ang), np.sin(ang)


def _fourier(f, g):
    b, l, w = f.shape
    n2 = 128
    n1 = l // n2
    scale = 1.0 / np.sqrt(float(l) * FOURIER_HEAD_DIM)
    c1, s1 = _dft_mats(n1)
    c2, s2 = _dft_mats(n2)
    cc, sc = _dft_mats(FOURIER_HEAD_DIM)
    eye = np.eye(FOURIER_HEADS)
    bc = np.kron(eye, cc) * scale
    bs = np.kron(eye, sc) * scale
    ang = 2.0 * np.pi * ((np.arange(n1)[:, None] * np.arange(n2)[None, :]) % l) / l
    tr = jnp.asarray(np.cos(ang)[:, :, None], F32)
    ti = jnp.asarray(-np.sin(ang)[:, :, None], F32)
    as_bf = lambda m: jnp.asarray(m, BF16)

    cb = min(8192, n2 * w)
    f2 = f.reshape(b, n1, n2 * w)
    blk = pl.BlockSpec((1, n1, cb), lambda i, j: (i, 0, j))
    mat1 = pl.BlockSpec((n1, n1), lambda i, j: (0, 0))
    shp = jax.ShapeDtypeStruct((b, n1, n2 * w), F32)
    ar, ai = pl.pallas_call(
        _dft_rows_kernel,
        grid=(b, n2 * w // cb),
        in_specs=[blk, mat1, mat1],
        out_specs=[blk, blk],
        out_shape=[shp, shp],
        compiler_params=_params("parallel", "arbitrary"),
        name="dft_rows",
    )(f2, as_bf(c1), as_bf(s1))

    ar = ar.reshape(b, n1, n2, w)
    ai = ai.reshape(b, n1, n2, w)
    a_blk = pl.BlockSpec((1, 1, n2, w), lambda i, j: (i, j, 0, 0))
    t_blk = pl.BlockSpec((1, n2, 1), lambda i, j: (j, 0, 0))
    mat2 = pl.BlockSpec((n2, n2), lambda i, j: (0, 0))
    matw = pl.BlockSpec((w, w), lambda i, j: (0, 0))
    out = pl.pallas_call(
        _dft_cols_kernel,
        grid=(b, n1),
        in_specs=[a_blk, a_blk, t_blk, t_blk, mat2, mat2, matw, matw,
                  pl.BlockSpec((1, w), lambda i, j: (0, 0))],
        out_specs=pl.BlockSpec((1, n2, w), lambda i, j: (i, 0, j)),
        out_shape=jax.ShapeDtypeStruct((b, n2, n1 * w), BF16),
        compiler_params=_params("parallel", "arbitrary"),
        name="dft_cols",
    )(ar, ai, tr, ti, as_bf(c2), as_bf(s2), as_bf(bc), as_bf(bs), g.reshape(1, w))
    return out.reshape(b, l, w)


def _mixout_kernel(hf_ref, hb_ref, gg_ref, fm_ref, x_ref, g1_ref, sh2_ref, sc2_ref, lg_ref,
                   wo_f_ref, wo_r_ref, n2g_ref, wq_hi_ref, wq_lo_ref, x1_ref, h2_ref, q_ref):
    rx = (hf_ref[0] + hb_ref[0]) * jax.nn.gelu(gg_ref[0])
    rxn = _rmsnorm(rx, lg_ref[...])
    mx = _dot(fm_ref[0], wo_f_ref[...]) + _dot(rxn.astype(BF16), wo_r_ref[...])
    x1 = x_ref[0] + g1_ref[0] * mx
    x1_ref[0] = x1
    h2 = _rmsnorm(x1, n2g_ref[...]) * (1.0 + sc2_ref[0]) + sh2_ref[0]
    h2_ref[0] = h2
    q_ref[0] = _dot3(h2, wq_hi_ref[...], wq_lo_ref[...])


def _mix_out(hf, hb, gg, fm, x, g1, sh2, sc2, lru_g, wo_f, wo_r, n2g, wq_hi, wq_lo, tm):
    b, l, d = x.shape
    nq = wq_hi.shape[1]
    half = pl.BlockSpec((1, tm, LRU_WIDTH), lambda i, j: (i, j, 0))
    tok = pl.BlockSpec((1, tm, d), lambda i, j: (i, j, 0))
    row = pl.BlockSpec((1, 1, d), lambda i, j: (i, 0, 0))
    full = lambda shape: pl.BlockSpec(shape, lambda i, j: (0,) * len(shape))
    return pl.pallas_call(
        _mixout_kernel,
        grid=(b, l // tm),
        in_specs=[half, half, half, half, tok, row, row, row, full((1, LRU_WIDTH)),
                  full((FOURIER_WIDTH, d)), full((LRU_WIDTH, d)), full((1, d)),
                  full((d, nq)), full((d, nq))],
        out_specs=[tok, tok, pl.BlockSpec((1, tm, nq), lambda i, j: (i, j, 0))],
        out_shape=[jax.ShapeDtypeStruct((b, l, d), F32), jax.ShapeDtypeStruct((b, l, d), F32),
                   jax.ShapeDtypeStruct((b, l, nq), F32)],
        compiler_params=_params("parallel", "arbitrary"),
        name="mix_out",
    )(hf, hb, gg, fm, x, g1, sh2, sc2, lru_g, wo_f, wo_r, n2g, wq_hi, wq_lo)


def _top16(s, payload=None):
    n = s.shape[0]
    iota = lax.broadcasted_iota(jnp.int32, s.shape, 0).astype(F32)
    vals, picks = [], []
    for _ in range(PEER_TOPK):
        m = jnp.max(s, axis=0, keepdims=True)
        pos = jnp.min(jnp.where(s == m, iota, float(n)), axis=0, keepdims=True)
        hit = iota == pos
        vals.append(m)
        if payload is None:
            picks.append(pos)
        else:
            picks.append(jnp.max(jnp.where(hit, payload, -1.0), axis=0, keepdims=True))
        s = jnp.where(hit, -jnp.inf, s)
    return jnp.concatenate(vals, axis=0), jnp.concatenate(picks, axis=0)


def _candidate_grid(s1, i1, s2, i2):
    k = PEER_TOPK
    row8 = lax.broadcasted_iota(jnp.int32, (8, s1.shape[1]), 0)
    vals = [s1[0:1] + s2, s1[1:2] + s2[0:8]]
    idxs = [i1[0:1] * PEER_NKEYS + i2, i1[1:2] * PEER_NKEYS + i2[0:8]]
    for i in range(2, 8):
        ok = row8 < k // (i + 1)
        vals.append(jnp.where(ok, s1[i:i + 1] + s2[0:8], -jnp.inf))
        idxs.append(i1[i:i + 1] * PEER_NKEYS + i2[0:8])
    vals.append(s1[8:16] + s2[0:1])
    idxs.append(i1[8:16] * PEER_NKEYS + i2[0:1])
    return jnp.concatenate(vals, axis=0), jnp.concatenate(idxs, axis=0)


def _topk_kernel(q_ref, k_hi_ref, k_lo_ref, idx_ref, g_ref):
    nt = (((1,), (1,)), ((), ()))
    idx_rows, gate_rows = [], []
    for h in range(PEER_HEADS):
        tops = []
        for p in range(2):
            hp = 2 * h + p
            qs = q_ref[:, hp * PEER_KEY_HALF:(hp + 1) * PEER_KEY_HALF]
            q_hi, q_lo = _split_bf16(qs)
            k_hi = k_hi_ref[hp]
            k_lo = k_lo_ref[hp]
            dg = lambda a, b: lax.dot_general(a, b, nt, preferred_element_type=F32)
            s = dg(k_hi, q_hi) + (dg(k_hi, q_lo) + dg(k_lo, q_hi))
            tops.append(_top16(s))
        (s1, i1), (s2, i2) = tops
        cand, cidx = _candidate_grid(s1, i1, s2, i2)
        sc, idx = _top16(cand, cidx)
        e = jnp.exp(sc - sc[0:1])
        g = e / jnp.sum(e, axis=0, keepdims=True)
        idx_rows.append(idx)
        gate_rows.append(g)
    idx_ref[...] = jnp.concatenate(idx_rows, axis=0).T.astype(jnp.int32)
    g_ref[...] = jnp.concatenate(gate_rows, axis=0).T


def _peer_topk(q, k_hi, k_lo, tm):
    t, nq = q.shape
    nsel = PEER_HEADS * PEER_TOPK
    full = pl.BlockSpec(k_hi.shape, lambda i: (0, 0, 0))
    out = pl.BlockSpec((tm, nsel), lambda i: (i, 0))
    return pl.pallas_call(
        _topk_kernel,
        grid=(t // tm,),
        in_specs=[pl.BlockSpec((tm, nq), lambda i: (i, 0)), full, full],
        out_specs=[out, out],
        out_shape=[jax.ShapeDtypeStruct((t, nsel), jnp.int32), jax.ShapeDtypeStruct((t, nsel), F32)],
        compiler_params=_params("parallel"),
        name="peer_topk",
    )(q, k_hi, k_lo)


PEER_TB = 8
PEER_ROWS = PEER_TB * PEER_HEADS * PEER_TOPK
SLAB = D_MODEL // 128
PEER_PITCH = 24


def _peer_kernel(idx_first, idx_a, idx_b, uv_hbm, h2_ref, gt_ref, x1_ref, g2_ref, fg_ref, o_ref,
                 buf0, buf1, sem):
    j = pl.program_id(0)
    n = pl.num_programs(0)
    nsel = PEER_HEADS * PEER_TOPK
    bufs = (buf0, buf1)

    def row_copy(e, s, r):
        return pltpu.make_async_copy(uv_hbm.at[e], bufs[s].at[pl.ds(r * PEER_PITCH, 2 * SLAB)], sem.at[s])

    def wait_slot(s):
        done = bufs[s].at[pl.ds(0, PEER_ROWS * 2 * SLAB)]
        pltpu.make_async_copy(done, done, sem.at[s]).wait()

    @pl.when(j == 0)
    def _():
        def body(r, carry):
            dst = bufs[0].at[pl.ds(pl.multiple_of(r * PEER_PITCH, 8), 2 * SLAB)]
            pltpu.make_async_copy(uv_hbm.at[idx_first[0, r]], dst, sem.at[0]).start()
            return carry
        lax.fori_loop(0, PEER_ROWS, body, 0, unroll=8)

    g2 = g2_ref[0]
    fg = fg_ref[...]
    for s, idx_next in ((0, idx_a), (1, idx_b)):
        wait_slot(s)
        for r in range(PEER_ROWS):
            row_copy(idx_next[0, r], 1 - s, r).start(priority=r % 2)
        gcol = gt_ref[s * PEER_TB:(s + 1) * PEER_TB, :].T
        rows = []
        for t in range(PEER_TB):
            def chunk(c):
                return bufs[s][pl.ds(t * nsel * PEER_PITCH + c, nsel, stride=PEER_PITCH), :]
            u = jnp.concatenate([chunk(c) for c in range(SLAB)], axis=1)
            v = jnp.concatenate([chunk(SLAB + c) for c in range(SLAB)], axis=1)
            xt = h2_ref[s * PEER_TB + t:s * PEER_TB + t + 1, :]
            act = jnp.sum(u * xt, axis=-1, keepdims=True)
            w = gcol[:, t:t + 1] * jax.nn.gelu(act)
            rows.append(jnp.sum(v * w, axis=0, keepdims=True))
        y = jnp.concatenate(rows, axis=0)
        tok = slice(s * PEER_TB, (s + 1) * PEER_TB)
        o_ref[tok, :] = _rmsnorm(x1_ref[tok, :] + g2 * y, fg)

    @pl.when(j == n - 1)
    def _():
        wait_slot(0)


def _peer_ffn(idx, gate_t, uv, h2, x1, g2, fg, seq_len):
    t, d = h2.shape
    nb = t // PEER_TB
    nsel = PEER_HEADS * PEER_TOPK
    idx3 = idx.reshape(nb, 1, PEER_ROWS)
    smem = lambda f: pl.BlockSpec((None, 1, PEER_ROWS), f, memory_space=pltpu.SMEM)
    tok = pl.BlockSpec((2 * PEER_TB, d), lambda j: (j, 0))
    return pl.pallas_call(
        _peer_kernel,
        grid=(nb // 2,),
        in_specs=[smem(lambda j: (0, 0, 0)),
                  smem(lambda j: (2 * j + 1, 0, 0)),
                  smem(lambda j: (jnp.minimum(2 * j + 2, nb - 1), 0, 0)),
                  pl.BlockSpec(memory_space=pl.ANY),
                  tok,
                  pl.BlockSpec((2 * PEER_TB, nsel), lambda j: (j, 0)),
                  tok,
                  pl.BlockSpec((1, 1, d), lambda j: (j * 2 * PEER_TB // seq_len, 0, 0)),
                  pl.BlockSpec((1, d), lambda j: (0, 0))],
        out_specs=tok,
        out_shape=jax.ShapeDtypeStruct((t, d), F32),
        scratch_shapes=[pltpu.VMEM((PEER_ROWS * PEER_PITCH, 128), F32),
                        pltpu.VMEM((PEER_ROWS * PEER_PITCH, 128), F32),
                        pltpu.SemaphoreType.DMA((2,))],
        compiler_params=_params("arbitrary"),
        name="peer_ffn",
    )(idx3, idx3, idx3, uv, h2, gate_t, x1, g2, fg)


def _block_diag(w):
    h, dh, _ = w.shape
    eye = jnp.eye(h, dtype=w.dtype)
    return (eye[:, None, :, None] * w[:, :, None, :]).reshape(h * dh, h * dh)


def kernel(x, c, ctx, c_ctx, w_mod, b_mod, norm1_g, w_in, conv_w, conv_b, lru_w_a, lru_b_a, lru_w_x,
           lru_b_x, lru_lambda, fourier_out_g, lru_out_g, w_out, norm2_g, peer_w_q, peer_sub_keys,
           peer_u, peer_v, final_norm_g):
    b, l, d = x.shape
    depth = w_mod.shape[0]
    assert depth == 1, "context stream update between layers is not implemented"
    assert d == D_MODEL and l % 1024 == 0 and b <= 7
    lyr = 0
    c_len = ctx.shape[1]

    cvec = jnp.zeros((8, d), F32).at[:b].set(c).at[b].set(c_ctx)
    mod = _mod(cvec, w_mod[lyr], b_mod[lyr])
    mod_x = mod[:b].reshape(b, N_MOD, 1, d)
    sh1, sc1, g1, sh2, sc2, g2 = [mod_x[:, k] for k in range(N_MOD)]
    mod_c = jnp.broadcast_to(mod[b].reshape(1, N_MOD, 1, d), (b, N_MOD, 1, d))

    n1g = norm1_g[lyr].reshape(1, d)
    w_in_b = w_in[lyr].astype(BF16)
    fx, ux, gx = _in_proj(x, sh1, sc1, n1g, w_in_b, tm=512)
    _, uc, _ = _in_proj(ctx, mod_c[:, 0], mod_c[:, 1], n1g, w_in_b, tm=c_len)

    wa = jnp.stack([_block_diag(lru_w_a[lyr, k]) for k in range(2)]).astype(BF16)
    wx = jnp.stack([_block_diag(lru_w_x[lyr, k]) for k in range(2)]).astype(BF16)
    lru_args = (conv_w[lyr], conv_b[lyr], wa, lru_b_a[lyr], wx, lru_b_x[lyr], lru_lambda[lyr])
    h_zero = jnp.zeros((b, 2, LRU_WIDTH), F32)
    _, _, fin = _lru(uc, *lru_args, h_zero, row_len=c_len, t=c_len)
    hf, hb, _ = _lru(ux, *lru_args, fin, row_len=GRID_W, t=256)

    fm = _fourier(fx, fourier_out_g[lyr])

    w_out_b = w_out[lyr].astype(BF16)
    wq_hi, wq_lo = _split_bf16(peer_w_q[lyr])
    x1, h2, q = _mix_out(hf, hb, gx, fm, x, g1, sh2, sc2, lru_out_g[lyr].reshape(1, LRU_WIDTH),
                         w_out_b[:FOURIER_WIDTH], w_out_b[FOURIER_WIDTH:], norm2_g[lyr].reshape(1, d),
                         wq_hi, wq_lo, tm=512)

    t = b * l
    nsel = PEER_HEADS * PEER_TOPK
    keys = peer_sub_keys[lyr].reshape(2 * PEER_HEADS, PEER_NKEYS, PEER_KEY_HALF)
    k_hi, k_lo = _split_bf16(keys)
    idx, gate = _peer_topk(q.reshape(t, -1), k_hi, k_lo, tm=256)
    uv = jnp.concatenate([peer_u[lyr], peer_v[lyr]], axis=1).reshape(-1, 2 * SLAB, 128)
    out = _peer_ffn(idx, gate, uv, h2.reshape(t, d), x1.reshape(t, d), g2,
                    final_norm_g.reshape(1, d), seq_len=l)
    return out.reshape(b, l, d)
```

```python
import functools

import jax
import jax.numpy as jnp
import numpy as np
from jax import lax
from jax.experimental import pallas as pl
from jax.experimental.pallas import tpu as pltpu

F32 = jnp.float32
BF16 = jnp.bfloat16

D_MODEL = 1024
GRID_W = 64
FOURIER_WIDTH = 512
FOURIER_HEADS = 8
FOURIER_HEAD_DIM = FOURIER_WIDTH // FOURIER_HEADS
LRU_WIDTH = 512
LRU_HEADS = 8
IN_WIDTH = FOURIER_WIDTH + 2 * LRU_WIDTH
CONV_W = 4
LRU_C = 8.0
PEER_HEADS = 8
PEER_NKEYS = 128
PEER_KEY_HALF = 128
PEER_TOPK = 16
PEER_NSEL = PEER_HEADS * PEER_TOPK
N_MOD = 6
EPS = 1e-6

HALF_ROWS = D_MODEL // 2 // 128
TAB_PAD = 8
PEER_TOK = 16
PEER_VMEM = 52 * 1024 * 1024

VMEM_LIMIT = 48 * 1024 * 1024


def _params(*sem):
    return pltpu.CompilerParams(dimension_semantics=sem, vmem_limit_bytes=VMEM_LIMIT)


def _split_bf16(a):
    hi = a.astype(BF16)
    lo = (a - hi.astype(F32)).astype(BF16)
    return hi, lo


def _dot(a, b):
    return jnp.dot(a, b, preferred_element_type=F32)


def _dot3(a, b_hi, b_lo):
    a_hi, a_lo = _split_bf16(a)
    return _dot(a_hi, b_hi) + (_dot(a_hi, b_lo) + _dot(a_lo, b_hi))


def _rmsnorm(x, g):
    return x * lax.rsqrt(jnp.mean(x * x, axis=-1, keepdims=True) + EPS) * g


def _mod_kernel(c_ref, w_ref, b_ref, o_ref):
    c = c_ref[...]
    s = c * jax.nn.sigmoid(c)
    o_ref[...] = jnp.dot(s, w_ref[...], preferred_element_type=F32,
                         precision=lax.Precision.HIGHEST) + b_ref[...]


def _mod(cvec, w_mod, b_mod):
    n = w_mod.shape[1]
    bn = 768
    return pl.pallas_call(
        _mod_kernel,
        grid=(n // bn,),
        in_specs=[pl.BlockSpec((8, D_MODEL), lambda j: (0, 0)),
                  pl.BlockSpec((D_MODEL, bn), lambda j: (0, j)),
                  pl.BlockSpec((1, bn), lambda j: (0, j))],
        out_specs=pl.BlockSpec((8, bn), lambda j: (0, j)),
        out_shape=jax.ShapeDtypeStruct((8, n), F32),
        compiler_params=_params("arbitrary"),
        name="mod",
    )(cvec, w_mod, b_mod.reshape(1, n))


def _inproj_kernel(x_ref, sh_ref, sc_ref, g_ref, w_ref, f_ref, u_ref, gg_ref):
    h = _rmsnorm(x_ref[0], g_ref[...]) * (1.0 + sc_ref[0]) + sh_ref[0]
    o = _dot(h.astype(BF16), w_ref[...])
    f_ref[0] = o[:, :FOURIER_WIDTH]
    u_ref[0] = o[:, FOURIER_WIDTH:FOURIER_WIDTH + LRU_WIDTH]
    gg_ref[0] = o[:, FOURIER_WIDTH + LRU_WIDTH:]


def _in_proj(x, sh, sc, g, w_bf16, tm):
    b, l, _ = x.shape
    row = pl.BlockSpec((1, 1, D_MODEL), lambda i, j: (i, 0, 0))
    out = pl.BlockSpec((1, tm, FOURIER_WIDTH), lambda i, j: (i, j, 0))
    shp = jax.ShapeDtypeStruct((b, l, FOURIER_WIDTH), F32)
    return pl.pallas_call(
        _inproj_kernel,
        grid=(b, l // tm),
        in_specs=[pl.BlockSpec((1, tm, D_MODEL), lambda i, j: (i, j, 0)), row, row,
                  pl.BlockSpec((1, D_MODEL), lambda i, j: (0, 0)),
                  pl.BlockSpec((D_MODEL, IN_WIDTH), lambda i, j: (0, 0))],
        out_specs=[out, out, out],
        out_shape=[shp, shp, shp],
        compiler_params=_params("parallel", "arbitrary"),
        name="in_proj",
    )(x, sh, sc, g, w_bf16)


def _conv_rows(x, w, b, row_len):
    t = x.shape[0]
    col = lax.broadcasted_iota(jnp.int32, x.shape, 0) % row_len
    y = b + x * w[CONV_W // 2:CONV_W // 2 + 1]
    for k in range(CONV_W):
        off = k - CONV_W // 2
        if off == 0:
            continue
        shifted = pltpu.roll(x, (-off) % t, axis=0)
        ok = (col + off >= 0) & (col + off < row_len)
        y = y + jnp.where(ok, shifted, 0.0) * w[k:k + 1]
    return y


def _lru_terms(xc, wa, ba, wx, bx, sp):
    xb = xc.astype(BF16)
    r = jax.nn.sigmoid(_dot(xb, wa) + ba)
    i = jax.nn.sigmoid(_dot(xb, wx) + bx)
    log_a = (-LRU_C) * r * sp
    a = jnp.exp(log_a)
    th = jnp.tanh(log_a)
    one_minus_a2 = (-2.0) * th / (1.0 - th)
    return a, jnp.sqrt(one_minus_a2) * (i * xc)


def _scan_block(a, bv, h0, reverse):
    t = a.shape[0]
    row = lax.broadcasted_iota(jnp.int32, a.shape, 0)
    d = 1
    while d < t:
        if reverse:
            ok = row < t - d
            shift = t - d
        else:
            ok = row >= d
            shift = d
        a_s = jnp.where(ok, pltpu.roll(a, shift, axis=0), 1.0)
        b_s = jnp.where(ok, pltpu.roll(bv, shift, axis=0), 0.0)
        bv = a * b_s + bv
        a = a * a_s
        d *= 2
    return a * h0 + bv


def _lru_kernel(uf_ref, ub_ref, cw_ref, cb_ref, wa_ref, ba_ref, wx_ref, bx_ref, lam_ref, h0_ref,
                hf_ref, hb_ref, fin_ref, carry, *, row_len):
    j = pl.program_id(1)

    @pl.when(j == 0)
    def _():
        carry[...] = h0_ref[0]

    cw = cw_ref[...]
    cb = cb_ref[...]
    t = uf_ref.shape[1]
    lam = lam_ref[...]
    sp = jax.nn.softplus(-lam)
    for d, (u_ref, o_ref) in enumerate(((uf_ref, hf_ref), (ub_ref, hb_ref))):
        xc = _conv_rows(u_ref[0], cw, cb, row_len)
        a, bv = _lru_terms(xc, wa_ref[d], ba_ref[d:d + 1], wx_ref[d], bx_ref[d:d + 1], sp[d:d + 1])
        h = _scan_block(a, bv, carry[d:d + 1], reverse=(d == 1))
        o_ref[0] = h
        last = h[0:1] if d == 1 else h[t - 1:t]
        carry[d:d + 1] = last
    fin_ref[0] = carry[...]


def _lru(u, conv_w, conv_b, wa_bd, ba, wx_bd, bx, lam, h0, row_len, t):
    b, l, c = u.shape
    nb = l // t
    kern = functools.partial(_lru_kernel, row_len=row_len)
    full = lambda shape: pl.BlockSpec(shape, lambda i, j: (0,) * len(shape))
    blk_f = pl.BlockSpec((1, t, c), lambda i, j: (i, j, 0))
    blk_b = pl.BlockSpec((1, t, c), lambda i, j: (i, nb - 1 - j, 0))
    st = pl.BlockSpec((1, 2, c), lambda i, j: (i, 0, 0))
    return pl.pallas_call(
        kern,
        grid=(b, nb),
        in_specs=[blk_f, blk_b, full((CONV_W, c)), full((1, c)), full((2, c, c)), full((2, c)),
                  full((2, c, c)), full((2, c)), full((2, c)), st],
        out_specs=[blk_f, blk_b, st],
        out_shape=[jax.ShapeDtypeStruct((b, l, c), F32), jax.ShapeDtypeStruct((b, l, c), F32),
                   jax.ShapeDtypeStruct((b, 2, c), F32)],
        scratch_shapes=[pltpu.VMEM((2, c), F32)],
        compiler_params=_params("parallel", "arbitrary"),
        name="lru",
    )(u, u, conv_w, conv_b.reshape(1, c), wa_bd, ba, wx_bd, bx, lam, h0)


def _dft_rows_kernel(f_ref, c_ref, s_ref, ar_ref, ai_ref):
    x = f_ref[0].astype(BF16)
    ar_ref[0] = _dot(c_ref[...], x)
    ai_ref[0] = -_dot(s_ref[...], x)


def _dft_cols_kernel(ar_ref, ai_ref, tr_ref, ti_ref, c_ref, s_ref, bc_ref, bs_ref, g_ref, o_ref):
    ar = ar_ref[0, 0]
    ai = ai_ref[0, 0]
    tr = tr_ref[0]
    ti = ti_ref[0]
    br = (ar * tr - ai * ti).astype(BF16)
    bi = (ar * ti + ai * tr).astype(BF16)
    c = c_ref[...]
    s = s_ref[...]
    zr = _dot(c, br) + _dot(s, bi)
    zi = _dot(c, bi) - _dot(s, br)
    y = _dot(zr.astype(BF16), bc_ref[...]) + _dot(zi.astype(BF16), bs_ref[...])
    o_ref[0] = _rmsnorm(y, g_ref[...]).astype(o_ref.dtype)


def _dft_mats(n):
    k = np.arange(n)
    ang = 2.0 * np.pi * ((k[:, None] * k[None, :]) % n) / n
    return np.cos(ang), np.sin(ang)


def _fourier(f, g):
    b, l, w = f.shape
    n2 = 128
    n1 = l // n2
    scale = 1.0 / np.sqrt(float(l) * FOURIER_HEAD_DIM)
    c1, s1 = _dft_mats(n1)
    c2, s2 = _dft_mats(n2)
    cc, sc = _dft_mats(FOURIER_HEAD_DIM)
    eye = np.eye(FOURIER_HEADS)
    bc = np.kron(eye, cc) * scale
    bs = np.kron(eye, sc) * scale
    ang = 2.0 * np.pi * ((np.arange(n1)[:, None] * np.arange(n2)[None, :]) % l) / l
    tr = jnp.asarray(np.cos(ang)[:, :, None], F32)
    ti = jnp.asarray(-np.sin(ang)[:, :, None], F32)
    as_bf = lambda m: jnp.asarray(m, BF16)

    cb = min(8192, n2 * w)
    f2 = f.reshape(b, n1, n2 * w)
    blk = pl.BlockSpec((1, n1, cb), lambda i, j: (i, 0, j))
    mat1 = pl.BlockSpec((n1, n1), lambda i, j: (0, 0))
    shp = jax.ShapeDtypeStruct((b, n1, n2 * w), F32)
    ar, ai = pl.pallas_call(
        _dft_rows_kernel,
        grid=(b, n2 * w // cb),
        in_specs=[blk, mat1, mat1],
        out_specs=[blk, blk],
        out_shape=[shp, shp],
        compiler_params=_params("parallel", "arbitrary"),
        name="dft_rows",
    )(f2, as_bf(c1), as_bf(s1))

    ar = ar.reshape(b, n1, n2, w)
    ai = ai.reshape(b, n1, n2, w)
    a_blk = pl.BlockSpec((1, 1, n2, w), lambda i, j: (i, j, 0, 0))
    t_blk = pl.BlockSpec((1, n2, 1), lambda i, j: (j, 0, 0))
    mat2 = pl.BlockSpec((n2, n2), lambda i, j: (0, 0))
    matw = pl.BlockSpec((w, w), lambda i, j: (0, 0))
    out = pl.pallas_call(
        _dft_cols_kernel,
        grid=(b, n1),
        in_specs=[a_blk, a_blk, t_blk, t_blk, mat2, mat2, matw, matw,
                  pl.BlockSpec((1, w), lambda i, j: (0, 0))],
        out_specs=pl.BlockSpec((1, n2, w), lambda i, j: (i, 0, j)),
        out_shape=jax.ShapeDtypeStruct((b, n2, n1 * w), BF16),
        compiler_params=_params("parallel", "arbitrary"),
        name="dft_cols",
    )(ar, ai, tr, ti, as_bf(c2), as_bf(s2), as_bf(bc), as_bf(bs), g.reshape(1, w))
    return out.reshape(b, l, w)


def _mixout_kernel(hf_ref, hb_ref, gg_ref, fm_ref, x_ref, g1_ref, sh2_ref, sc2_ref, lg_ref,
                   wo_f_ref, wo_r_ref, n2g_ref, wq_hi_ref, wq_lo_ref, x1_ref, h2_ref, q_ref):
    rx = (hf_ref[0] + hb_ref[0]) * jax.nn.gelu(gg_ref[0])
    rxn = _rmsnorm(rx, lg_ref[...])
    mx = _dot(fm_ref[0], wo_f_ref[...]) + _dot(rxn.astype(BF16), wo_r_ref[...])
    x1 = x_ref[0] + g1_ref[0] * mx
    x1_ref[0] = x1
    h2 = _rmsnorm(x1, n2g_ref[...]) * (1.0 + sc2_ref[0]) + sh2_ref[0]
    h2_ref[0] = h2
    q_ref[0] = _dot3(h2, wq_hi_ref[...], wq_lo_ref[...])


def _mix_out(hf, hb, gg, fm, x, g1, sh2, sc2, lru_g, wo_f, wo_r, n2g, wq_hi, wq_lo, tm):
    b, l, d = x.shape
    nq = wq_hi.shape[1]
    half = pl.BlockSpec((1, tm, LRU_WIDTH), lambda i, j: (i, j, 0))
    tok = pl.BlockSpec((1, tm, d), lambda i, j: (i, j, 0))
    row = pl.BlockSpec((1, 1, d), lambda i, j: (i, 0, 0))
    full = lambda shape: pl.BlockSpec(shape, lambda i, j: (0,) * len(shape))
    return pl.pallas_call(
        _mixout_kernel,
        grid=(b, l // tm),
        in_specs=[half, half, half, half, tok, row, row, row, full((1, LRU_WIDTH)),
                  full((FOURIER_WIDTH, d)), full((LRU_WIDTH, d)), full((1, d)),
                  full((d, nq)), full((d, nq))],
        out_specs=[tok, tok, pl.BlockSpec((1, tm, nq), lambda i, j: (i, j, 0))],
        out_shape=[jax.ShapeDtypeStruct((b, l, d), F32), jax.ShapeDtypeStruct((b, l, d), F32),
                   jax.ShapeDtypeStruct((b, l, nq), F32)],
        compiler_params=_params("parallel", "arbitrary"),
        name="mix_out",
    )(hf, hb, gg, fm, x, g1, sh2, sc2, lru_g, wo_f, wo_r, n2g, wq_hi, wq_lo)


def _top16(s, payload=None):
    n = s.shape[0]
    iota = lax.broadcasted_iota(jnp.int32, s.shape, 0).astype(F32)
    vals, picks = [], []
    for _ in range(PEER_TOPK):
        m = jnp.max(s, axis=0, keepdims=True)
        pos = jnp.min(jnp.where(s == m, iota, float(n)), axis=0, keepdims=True)
        hit = iota == pos
        vals.append(m)
        if payload is None:
            picks.append(pos)
        else:
            picks.append(jnp.max(jnp.where(hit, payload, -1.0), axis=0, keepdims=True))
        s = jnp.where(hit, -jnp.inf, s)
    return jnp.concatenate(vals, axis=0), jnp.concatenate(picks, axis=0)


def _candidate_grid(s1, i1, s2, i2):
    k = PEER_TOPK
    row8 = lax.broadcasted_iota(jnp.int32, (8, s1.shape[1]), 0)
    vals = [s1[0:1] + s2, s1[1:2] + s2[0:8]]
    idxs = [i1[0:1] * PEER_NKEYS + i2, i1[1:2] * PEER_NKEYS + i2[0:8]]
    for i in range(2, 8):
        ok = row8 < k // (i + 1)
        vals.append(jnp.where(ok, s1[i:i + 1] + s2[0:8], -jnp.inf))
        idxs.append(i1[i:i + 1] * PEER_NKEYS + i2[0:8])
    vals.append(s1[8:16] + s2[0:1])
    idxs.append(i1[8:16] * PEER_NKEYS + i2[0:1])
    return jnp.concatenate(vals, axis=0), jnp.concatenate(idxs, axis=0)


def _topk_kernel(q_ref, k_hi_ref, k_lo_ref, idx_ref, g_ref):
    nt = (((1,), (1,)), ((), ()))
    idx_rows, gate_rows = [], []
    for h in range(PEER_HEADS):
        tops = []
        for p in range(2):
            hp = 2 * h + p
            qs = q_ref[:, hp * PEER_KEY_HALF:(hp + 1) * PEER_KEY_HALF]
            q_hi, q_lo = _split_bf16(qs)
            k_hi = k_hi_ref[hp]
            k_lo = k_lo_ref[hp]
            dg = lambda a, b: lax.dot_general(a, b, nt, preferred_element_type=F32)
            s = dg(k_hi, q_hi) + (dg(k_hi, q_lo) + dg(k_lo, q_hi))
            tops.append(_top16(s))
        (s1, i1), (s2, i2) = tops
        cand, cidx = _candidate_grid(s1, i1, s2, i2)
        sc, idx = _top16(cand, cidx)
        e = jnp.exp(sc - sc[0:1])
        g = e / jnp.sum(e, axis=0, keepdims=True)
        idx_rows.append(idx)
        gate_rows.append(g)
    rows = jnp.concatenate(idx_rows, axis=0) * float(HALF_ROWS) + float(TAB_PAD)
    idx_ref[...] = rows.T.astype(jnp.int32)
    g_ref[...] = jnp.concatenate(gate_rows, axis=0).T


def _peer_topk(q, k_hi, k_lo, tm):
    t, nq = q.shape
    nsel = PEER_NSEL
    full = pl.BlockSpec(k_hi.shape, lambda i: (0, 0, 0))
    out = pl.BlockSpec((tm, nsel), lambda i: (i, 0))
    return pl.pallas_call(
        _topk_kernel,
        grid=(t // tm,),
        in_specs=[pl.BlockSpec((tm, nq), lambda i: (i, 0)), full, full],
        out_specs=[out, out],
        out_shape=[jax.ShapeDtypeStruct((t, nsel), jnp.int32), jax.ShapeDtypeStruct((t, nsel), F32)],
        compiler_params=_params("parallel"),
        name="peer_topk",
    )(q, k_hi, k_lo)


def _pack_table(w):
    e, d = w.shape
    bits = lax.bitcast_convert_type(w.astype(BF16), jnp.uint16).astype(jnp.uint32)
    words = bits[:, :d // 2] | (bits[:, d // 2:] << 16)
    return jnp.pad(words.reshape(e * HALF_ROWS, 128), ((TAB_PAD, TAB_PAD), (0, 0)))


def _halves(words):
    lo = pltpu.bitcast(words << 16, F32)
    hi = pltpu.bitcast(words & jnp.uint32(0xFFFF0000), F32)
    return lo, hi


def _slab_pair(row_ref, t):
    def tile(base):
        rows = [row_ref[t:t + 1, base + c * 128:base + (c + 1) * 128] for c in range(HALF_ROWS)]
        return jnp.concatenate(rows + rows, axis=0)
    return tile(0), tile(D_MODEL // 2)


def _pair_words(tab, row_a, row_b):
    sub = lax.broadcasted_iota(jnp.int32, (8, 128), 0)
    wa = tab[pl.ds(pl.multiple_of(row_a, HALF_ROWS), 8), :]
    wb = tab[pl.ds(pl.multiple_of(row_b - HALF_ROWS, HALF_ROWS), 8), :]
    return jnp.where(sub < HALF_ROWS, wa, wb)


def _half_sums(zs):
    sub = lax.broadcasted_iota(jnp.int32, (8, 128), 0)
    z = [zs[0], zs[2], zs[1], zs[3]]
    for d in (2, 1):
        low = (sub & d) == 0
        z = [jnp.where(low, a + pltpu.roll(a, 8 - d, axis=0), b + pltpu.roll(b, d, axis=0))
             for a, b in zip(z[0::2], z[1::2])]
    return z[0]


def _half_spread(x):
    sub = lax.broadcasted_iota(jnp.int32, (8, 128), 0)
    even = (sub & 1) == 0
    lowp = (sub & 2) == 0
    ev = jnp.where(even, x, pltpu.roll(x, 1, axis=0))
    od = jnp.where(even, pltpu.roll(x, 7, axis=0), x)
    out = []
    for y in (ev, od):
        out.append((jnp.where(lowp, y, pltpu.roll(y, 2, axis=0)),
                    jnp.where(lowp, pltpu.roll(y, 6, axis=0), y)))
    return [out[0][0], out[1][0], out[0][1], out[1][1]]


def _load_table(tab_hbm, tab, sem):
    @pl.when(pl.program_id(0) == 0)
    def _():
        cp = pltpu.make_async_copy(tab_hbm, tab, sem.at[0])
        cp.start()
        cp.wait()


def _act_kernel(idx_ref, tab_hbm, h2_ref, gate_ref, w_ref, tab, sem):
    _load_table(tab_hbm, tab, sem)
    gcol = gate_ref[...].T
    cols = []
    for t in range(PEER_TOK):
        xa, xb = _slab_pair(h2_ref, t)
        sums = []
        for g in range(PEER_NSEL // 8):
            zs = []
            for j in range(4):
                base = t * PEER_NSEL + g * 8 + j
                lo, hi = _halves(_pair_words(tab, idx_ref[0, base], idx_ref[0, base + 4]))
                zs.append(lo * xa + hi * xb)
            sums.append(_half_sums(zs))
        act = jnp.sum(jnp.concatenate(sums, axis=0), axis=-1, keepdims=True)
        cols.append(gcol[:, t:t + 1] * jax.nn.gelu(act))
    w_ref[...] = jnp.concatenate(cols, axis=1).T


def _peer_act(rows, tab_u, h2, gate):
    t, d = h2.shape
    nb = t // PEER_TOK
    n = PEER_TOK * PEER_NSEL
    sel = pl.BlockSpec((PEER_TOK, PEER_NSEL), lambda i: (i, 0))
    return pl.pallas_call(
        _act_kernel,
        grid=(nb,),
        in_specs=[pl.BlockSpec((None, 1, n), lambda i: (i, 0, 0), memory_space=pltpu.SMEM),
                  pl.BlockSpec(memory_space=pl.ANY),
                  pl.BlockSpec((PEER_TOK, d), lambda i: (i, 0)),
                  sel],
        out_specs=sel,
        out_shape=jax.ShapeDtypeStruct((t, PEER_NSEL), F32),
        scratch_shapes=[pltpu.VMEM(tab_u.shape, jnp.uint32), pltpu.SemaphoreType.DMA((1,))],
        compiler_params=pltpu.CompilerParams(dimension_semantics=("arbitrary",), vmem_limit_bytes=PEER_VMEM),
        name="peer_act",
    )(rows.reshape(nb, 1, n), tab_u, h2, gate)


def _out_kernel(idx_ref, tab_hbm, w_ref, x1_ref, g2_ref, fg_ref, o_ref, tab, sem):
    _load_table(tab_hbm, tab, sem)
    wcol = w_ref[...].T
    outs = []
    for t in range(PEER_TOK):
        wb = jnp.broadcast_to(wcol[:, t:t + 1], (PEER_NSEL, 128))
        acc = [[None, None] for _ in range(4)]
        for g in range(PEER_NSEL // 8):
            spread = _half_spread(wb[g * 8:(g + 1) * 8, :])
            for j in range(4):
                base = t * PEER_NSEL + g * 8 + j
                lo, hi = _halves(_pair_words(tab, idx_ref[0, base], idx_ref[0, base + 4]))
                a = acc[j]
                a[0] = spread[j] * lo if a[0] is None else a[0] + spread[j] * lo
                a[1] = spread[j] * hi if a[1] is None else a[1] + spread[j] * hi
        lo = (acc[0][0] + acc[1][0]) + (acc[2][0] + acc[3][0])
        hi = (acc[0][1] + acc[1][1]) + (acc[2][1] + acc[3][1])
        lo = lo + pltpu.roll(lo, HALF_ROWS, axis=0)
        hi = hi + pltpu.roll(hi, HALF_ROWS, axis=0)
        outs.append(jnp.concatenate([lo[c:c + 1] for c in range(HALF_ROWS)] +
                                    [hi[c:c + 1] for c in range(HALF_ROWS)], axis=1))
    y = jnp.concatenate(outs, axis=0)
    o_ref[...] = _rmsnorm(x1_ref[...] + g2_ref[0] * y, fg_ref[...])


def _peer_out(rows, w, tab_v, x1, g2, fg, seq_len):
    t, d = x1.shape
    nb = t // PEER_TOK
    n = PEER_TOK * PEER_NSEL
    tok = pl.BlockSpec((PEER_TOK, d), lambda i: (i, 0))
    return pl.pallas_call(
        _out_kernel,
        grid=(nb,),
        in_specs=[pl.BlockSpec((None, 1, n), lambda i: (i, 0, 0), memory_space=pltpu.SMEM),
                  pl.BlockSpec(memory_space=pl.ANY),
                  pl.BlockSpec((PEER_TOK, PEER_NSEL), lambda i: (i, 0)),
                  tok,
                  pl.BlockSpec((1, 1, d), lambda i: (i * PEER_TOK // seq_len, 0, 0)),
                  pl.BlockSpec((1, d), lambda i: (0, 0))],
        out_specs=tok,
        out_shape=jax.ShapeDtypeStruct((t, d), F32),
        scratch_shapes=[pltpu.VMEM(tab_v.shape, jnp.uint32), pltpu.SemaphoreType.DMA((1,))],
        compiler_params=pltpu.CompilerParams(dimension_semantics=("arbitrary",), vmem_limit_bytes=PEER_VMEM),
        name="peer_out",
    )(rows.reshape(nb, 1, n), tab_v, w, x1, g2, fg)


def _block_diag(w):
    h, dh, _ = w.shape
    eye = jnp.eye(h, dtype=w.dtype)
    return (eye[:, None, :, None] * w[:, :, None, :]).reshape(h * dh, h * dh)


def kernel(x, c, ctx, c_ctx, w_mod, b_mod, norm1_g, w_in, conv_w, conv_b, lru_w_a, lru_b_a, lru_w_x,
           lru_b_x, lru_lambda, fourier_out_g, lru_out_g, w_out, norm2_g, peer_w_q, peer_sub_keys,
           peer_u, peer_v, final_norm_g):
    b, l, d = x.shape
    depth = w_mod.shape[0]
    assert depth == 1, "context stream update between layers is not implemented"
    assert d == D_MODEL and l % 1024 == 0 and b <= 7
    lyr = 0
    c_len = ctx.shape[1]

    cvec = jnp.zeros((8, d), F32).at[:b].set(c).at[b].set(c_ctx)
    mod = _mod(cvec, w_mod[lyr], b_mod[lyr])
    mod_x = mod[:b].reshape(b, N_MOD, 1, d)
    sh1, sc1, g1, sh2, sc2, g2 = [mod_x[:, k] for k in range(N_MOD)]
    mod_c = jnp.broadcast_to(mod[b].reshape(1, N_MOD, 1, d), (b, N_MOD, 1, d))

    n1g = norm1_g[lyr].reshape(1, d)
    w_in_b = w_in[lyr].astype(BF16)
    fx, ux, gx = _in_proj(x, sh1, sc1, n1g, w_in_b, tm=512)
    _, uc, _ = _in_proj(ctx, mod_c[:, 0], mod_c[:, 1], n1g, w_in_b, tm=c_len)

    wa = jnp.stack([_block_diag(lru_w_a[lyr, k]) for k in range(2)]).astype(BF16)
    wx = jnp.stack([_block_diag(lru_w_x[lyr, k]) for k in range(2)]).astype(BF16)
    lru_args = (conv_w[lyr], conv_b[lyr], wa, lru_b_a[lyr], wx, lru_b_x[lyr], lru_lambda[lyr])
    h_zero = jnp.zeros((b, 2, LRU_WIDTH), F32)
    _, _, fin = _lru(uc, *lru_args, h_zero, row_len=c_len, t=c_len)
    hf, hb, _ = _lru(ux, *lru_args, fin, row_len=GRID_W, t=256)

    fm = _fourier(fx, fourier_out_g[lyr])

    w_out_b = w_out[lyr].astype(BF16)
    wq_hi, wq_lo = _split_bf16(peer_w_q[lyr])
    x1, h2, q = _mix_out(hf, hb, gx, fm, x, g1, sh2, sc2, lru_out_g[lyr].reshape(1, LRU_WIDTH),
                         w_out_b[:FOURIER_WIDTH], w_out_b[FOURIER_WIDTH:], norm2_g[lyr].reshape(1, d),
                         wq_hi, wq_lo, tm=512)

    t = b * l
    keys = peer_sub_keys[lyr].reshape(2 * PEER_HEADS, PEER_NKEYS, PEER_KEY_HALF)
    k_hi, k_lo = _split_bf16(keys)
    rows, gate = _peer_topk(q.reshape(t, -1), k_hi, k_lo, tm=256)
    w = _peer_act(rows, _pack_table(peer_u[lyr]), h2.reshape(t, d), gate)
    out = _peer_out(rows, w, _pack_table(peer_v[lyr]), x1.reshape(t, d), g2,
                    final_norm_g.reshape(1, d), seq_len=l)
    return out.reshape(b, l, d)
```

```python
import functools

import jax
import jax.numpy as jnp
import numpy as np
from jax import lax
from jax.experimental import pallas as pl
from jax.experimental.pallas import tpu as pltpu

F32 = jnp.float32
BF16 = jnp.bfloat16

D_MODEL = 1024
GRID_W = 64
FOURIER_WIDTH = 512
FOURIER_HEADS = 8
FOURIER_HEAD_DIM = FOURIER_WIDTH // FOURIER_HEADS
LRU_WIDTH = 512
LRU_HEADS = 8
IN_WIDTH = FOURIER_WIDTH + 2 * LRU_WIDTH
CONV_W = 4
LRU_C = 8.0
PEER_HEADS = 8
PEER_NKEYS = 128
PEER_KEY_HALF = 128
PEER_TOPK = 16
PEER_NSEL = PEER_HEADS * PEER_TOPK
N_MOD = 6
EPS = 1e-6

HALF_ROWS = D_MODEL // 2 // 128
TAB_PAD = 8
PEER_TOK = 16
PEER_VMEM = 52 * 1024 * 1024

VMEM_LIMIT = 48 * 1024 * 1024


def _params(*sem):
    return pltpu.CompilerParams(dimension_semantics=sem, vmem_limit_bytes=VMEM_LIMIT)


def _split_bf16(a):
    hi = a.astype(BF16)
    lo = (a - hi.astype(F32)).astype(BF16)
    return hi, lo


def _dot(a, b):
    return jnp.dot(a, b, preferred_element_type=F32)


def _dot3(a, b_hi, b_lo):
    a_hi, a_lo = _split_bf16(a)
    return _dot(a_hi, b_hi) + (_dot(a_hi, b_lo) + _dot(a_lo, b_hi))


def _rmsnorm(x, g):
    return x * lax.rsqrt(jnp.mean(x * x, axis=-1, keepdims=True) + EPS) * g


def _mod_kernel(c_ref, w_ref, b_ref, o_ref):
    c = c_ref[...]
    s = c * jax.nn.sigmoid(c)
    o_ref[...] = jnp.dot(s, w_ref[...], preferred_element_type=F32,
                         precision=lax.Precision.HIGHEST) + b_ref[...]


def _mod(cvec, w_mod, b_mod):
    n = w_mod.shape[1]
    bn = 768
    return pl.pallas_call(
        _mod_kernel,
        grid=(n // bn,),
        in_specs=[pl.BlockSpec((8, D_MODEL), lambda j: (0, 0)),
                  pl.BlockSpec((D_MODEL, bn), lambda j: (0, j)),
                  pl.BlockSpec((1, bn), lambda j: (0, j))],
        out_specs=pl.BlockSpec((8, bn), lambda j: (0, j)),
        out_shape=jax.ShapeDtypeStruct((8, n), F32),
        compiler_params=_params("arbitrary"),
        name="mod",
    )(cvec, w_mod, b_mod.reshape(1, n))


def _inproj_kernel(x_ref, sh_ref, sc_ref, g_ref, w_ref, f_ref, u_ref, gg_ref):
    h = _rmsnorm(x_ref[0], g_ref[...]) * (1.0 + sc_ref[0]) + sh_ref[0]
    o = _dot(h.astype(BF16), w_ref[...])
    f_ref[0] = o[:, :FOURIER_WIDTH]
    u_ref[0] = o[:, FOURIER_WIDTH:FOURIER_WIDTH + LRU_WIDTH]
    gg_ref[0] = o[:, FOURIER_WIDTH + LRU_WIDTH:]


def _in_proj(x, sh, sc, g, w_bf16, tm):
    b, l, _ = x.shape
    row = pl.BlockSpec((1, 1, D_MODEL), lambda i, j: (i, 0, 0))
    out = pl.BlockSpec((1, tm, FOURIER_WIDTH), lambda i, j: (i, j, 0))
    shp = jax.ShapeDtypeStruct((b, l, FOURIER_WIDTH), F32)
    return pl.pallas_call(
        _inproj_kernel,
        grid=(b, l // tm),
        in_specs=[pl.BlockSpec((1, tm, D_MODEL), lambda i, j: (i, j, 0)), row, row,
                  pl.BlockSpec((1, D_MODEL), lambda i, j: (0, 0)),
                  pl.BlockSpec((D_MODEL, IN_WIDTH), lambda i, j: (0, 0))],
        out_specs=[out, out, out],
        out_shape=[shp, shp, shp],
        compiler_params=_params("parallel", "arbitrary"),
        name="in_proj",
    )(x, sh, sc, g, w_bf16)


def _conv_rows(x, w, b, row_len):
    t = x.shape[0]
    col = lax.broadcasted_iota(jnp.int32, x.shape, 0) % row_len
    y = b + x * w[CONV_W // 2:CONV_W // 2 + 1]
    for k in range(CONV_W):
        off = k - CONV_W // 2
        if off == 0:
            continue
        shifted = pltpu.roll(x, (-off) % t, axis=0)
        ok = (col + off >= 0) & (col + off < row_len)
        y = y + jnp.where(ok, shifted, 0.0) * w[k:k + 1]
    return y


def _lru_terms(xc, wa, ba, wx, bx, sp):
    xb = xc.astype(BF16)
    r = jax.nn.sigmoid(_dot(xb, wa) + ba)
    i = jax.nn.sigmoid(_dot(xb, wx) + bx)
    log_a = (-LRU_C) * r * sp
    a = jnp.exp(log_a)
    th = jnp.tanh(log_a)
    one_minus_a2 = (-2.0) * th / (1.0 - th)
    return a, jnp.sqrt(one_minus_a2) * (i * xc)


def _scan_block(a, bv, h0, reverse):
    t = a.shape[0]
    row = lax.broadcasted_iota(jnp.int32, a.shape, 0)
    d = 1
    while d < t:
        if reverse:
            ok = row < t - d
            shift = t - d
        else:
            ok = row >= d
            shift = d
        a_s = jnp.where(ok, pltpu.roll(a, shift, axis=0), 1.0)
        b_s = jnp.where(ok, pltpu.roll(bv, shift, axis=0), 0.0)
        bv = a * b_s + bv
        a = a * a_s
        d *= 2
    return a * h0 + bv


def _lru_kernel(uf_ref, ub_ref, cw_ref, cb_ref, wa_ref, ba_ref, wx_ref, bx_ref, lam_ref, h0_ref,
                hf_ref, hb_ref, fin_ref, carry, *, row_len):
    j = pl.program_id(1)

    @pl.when(j == 0)
    def _():
        carry[...] = h0_ref[0]

    cw = cw_ref[...]
    cb = cb_ref[...]
    t = uf_ref.shape[1]
    lam = lam_ref[...]
    sp = jax.nn.softplus(-lam)
    for d, (u_ref, o_ref) in enumerate(((uf_ref, hf_ref), (ub_ref, hb_ref))):
        xc = _conv_rows(u_ref[0], cw, cb, row_len)
        a, bv = _lru_terms(xc, wa_ref[d], ba_ref[d:d + 1], wx_ref[d], bx_ref[d:d + 1], sp[d:d + 1])
        h = _scan_block(a, bv, carry[d:d + 1], reverse=(d == 1))
        o_ref[0] = h
        last = h[0:1] if d == 1 else h[t - 1:t]
        carry[d:d + 1] = last
    fin_ref[0] = carry[...]


def _lru(u, conv_w, conv_b, wa_bd, ba, wx_bd, bx, lam, h0, row_len, t):
    b, l, c = u.shape
    nb = l // t
    kern = functools.partial(_lru_kernel, row_len=row_len)
    full = lambda shape: pl.BlockSpec(shape, lambda i, j: (0,) * len(shape))
    blk_f = pl.BlockSpec((1, t, c), lambda i, j: (i, j, 0))
    blk_b = pl.BlockSpec((1, t, c), lambda i, j: (i, nb - 1 - j, 0))
    st = pl.BlockSpec((1, 2, c), lambda i, j: (i, 0, 0))
    return pl.pallas_call(
        kern,
        grid=(b, nb),
        in_specs=[blk_f, blk_b, full((CONV_W, c)), full((1, c)), full((2, c, c)), full((2, c)),
                  full((2, c, c)), full((2, c)), full((2, c)), st],
        out_specs=[blk_f, blk_b, st],
        out_shape=[jax.ShapeDtypeStruct((b, l, c), F32), jax.ShapeDtypeStruct((b, l, c), F32),
                   jax.ShapeDtypeStruct((b, 2, c), F32)],
        scratch_shapes=[pltpu.VMEM((2, c), F32)],
        compiler_params=_params("parallel", "arbitrary"),
        name="lru",
    )(u, u, conv_w, conv_b.reshape(1, c), wa_bd, ba, wx_bd, bx, lam, h0)


def _dft_rows_kernel(f_ref, c_ref, s_ref, ar_ref, ai_ref):
    x = f_ref[0].astype(BF16)
    ar_ref[0] = _dot(c_ref[...], x)
    ai_ref[0] = -_dot(s_ref[...], x)


def _dft_cols_kernel(ar_ref, ai_ref, tr_ref, ti_ref, c_ref, s_ref, bc_ref, bs_ref, g_ref, o_ref):
    ar = ar_ref[0, 0]
    ai = ai_ref[0, 0]
    tr = tr_ref[0]
    ti = ti_ref[0]
    br = (ar * tr - ai * ti).astype(BF16)
    bi = (ar * ti + ai * tr).astype(BF16)
    c = c_ref[...]
    s = s_ref[...]
    zr = _dot(c, br) + _dot(s, bi)
    zi = _dot(c, bi) - _dot(s, br)
    y = _dot(zr.astype(BF16), bc_ref[...]) + _dot(zi.astype(BF16), bs_ref[...])
    o_ref[0] = _rmsnorm(y, g_ref[...]).astype(o_ref.dtype)


def _dft_mats(n):
    k = np.arange(n)
    ang = 2.0 * np.pi * ((k[:, None] * k[None, :]) % n) / n
    return np.cos(ang), np.sin(ang)


def _fourier(f, g):
    b, l, w = f.shape
    n2 = 128
    n1 = l // n2
    scale = 1.0 / np.sqrt(float(l) * FOURIER_HEAD_DIM)
    c1, s1 = _dft_mats(n1)
    c2, s2 = _dft_mats(n2)
    cc, sc = _dft_mats(FOURIER_HEAD_DIM)
    eye = np.eye(FOURIER_HEADS)
    bc = np.kron(eye, cc) * scale
    bs = np.kron(eye, sc) * scale
    ang = 2.0 * np.pi * ((np.arange(n1)[:, None] * np.arange(n2)[None, :]) % l) / l
    tr = jnp.asarray(np.cos(ang)[:, :, None], F32)
    ti = jnp.asarray(-np.sin(ang)[:, :, None], F32)
    as_bf = lambda m: jnp.asarray(m, BF16)

    cb = min(8192, n2 * w)
    f2 = f.reshape(b, n1, n2 * w)
    blk = pl.BlockSpec((1, n1, cb), lambda i, j: (i, 0, j))
    mat1 = pl.BlockSpec((n1, n1), lambda i, j: (0, 0))
    shp = jax.ShapeDtypeStruct((b, n1, n2 * w), F32)
    ar, ai = pl.pallas_call(
        _dft_rows_kernel,
        grid=(b, n2 * w // cb),
        in_specs=[blk, mat1, mat1],
        out_specs=[blk, blk],
        out_shape=[shp, shp],
        compiler_params=_params("parallel", "arbitrary"),
        name="dft_rows",
    )(f2, as_bf(c1), as_bf(s1))

    ar = ar.reshape(b, n1, n2, w)
    ai = ai.reshape(b, n1, n2, w)
    a_blk = pl.BlockSpec((1, 1, n2, w), lambda i, j: (i, j, 0, 0))
    t_blk = pl.BlockSpec((1, n2, 1), lambda i, j: (j, 0, 0))
    mat2 = pl.BlockSpec((n2, n2), lambda i, j: (0, 0))
    matw = pl.BlockSpec((w, w), lambda i, j: (0, 0))
    out = pl.pallas_call(
        _dft_cols_kernel,
        grid=(b, n1),
        in_specs=[a_blk, a_blk, t_blk, t_blk, mat2, mat2, matw, matw,
                  pl.BlockSpec((1, w), lambda i, j: (0, 0))],
        out_specs=pl.BlockSpec((1, n2, w), lambda i, j: (i, 0, j)),
        out_shape=jax.ShapeDtypeStruct((b, n2, n1 * w), BF16),
        compiler_params=_params("parallel", "arbitrary"),
        name="dft_cols",
    )(ar, ai, tr, ti, as_bf(c2), as_bf(s2), as_bf(bc), as_bf(bs), g.reshape(1, w))
    return out.reshape(b, l, w)


def _mixout_kernel(hf_ref, hb_ref, gg_ref, fm_ref, x_ref, g1_ref, sh2_ref, sc2_ref, lg_ref,
                   wo_f_ref, wo_r_ref, n2g_ref, wq_hi_ref, wq_lo_ref, x1_ref, h2_ref, q_ref):
    rx = (hf_ref[0] + hb_ref[0]) * jax.nn.gelu(gg_ref[0])
    rxn = _rmsnorm(rx, lg_ref[...])
    mx = _dot(fm_ref[0], wo_f_ref[...]) + _dot(rxn.astype(BF16), wo_r_ref[...])
    x1 = x_ref[0] + g1_ref[0] * mx
    x1_ref[0] = x1
    h2 = _rmsnorm(x1, n2g_ref[...]) * (1.0 + sc2_ref[0]) + sh2_ref[0]
    h2_ref[0] = h2
    q_ref[0] = _dot3(h2, wq_hi_ref[...], wq_lo_ref[...])


def _mix_out(hf, hb, gg, fm, x, g1, sh2, sc2, lru_g, wo_f, wo_r, n2g, wq_hi, wq_lo, tm):
    b, l, d = x.shape
    nq = wq_hi.shape[1]
    half = pl.BlockSpec((1, tm, LRU_WIDTH), lambda i, j: (i, j, 0))
    tok = pl.BlockSpec((1, tm, d), lambda i, j: (i, j, 0))
    row = pl.BlockSpec((1, 1, d), lambda i, j: (i, 0, 0))
    full = lambda shape: pl.BlockSpec(shape, lambda i, j: (0,) * len(shape))
    return pl.pallas_call(
        _mixout_kernel,
        grid=(b, l // tm),
        in_specs=[half, half, half, half, tok, row, row, row, full((1, LRU_WIDTH)),
                  full((FOURIER_WIDTH, d)), full((LRU_WIDTH, d)), full((1, d)),
                  full((d, nq)), full((d, nq))],
        out_specs=[tok, tok, pl.BlockSpec((1, tm, nq), lambda i, j: (i, j, 0))],
        out_shape=[jax.ShapeDtypeStruct((b, l, d), F32), jax.ShapeDtypeStruct((b, l, d), F32),
                   jax.ShapeDtypeStruct((b, l, nq), F32)],
        compiler_params=_params("parallel", "arbitrary"),
        name="mix_out",
    )(hf, hb, gg, fm, x, g1, sh2, sc2, lru_g, wo_f, wo_r, n2g, wq_hi, wq_lo)


def _top16(s, payload=None):
    n = s.shape[0]
    iota = lax.broadcasted_iota(jnp.int32, s.shape, 0).astype(F32)
    vals, picks = [], []
    for _ in range(PEER_TOPK):
        m = jnp.max(s, axis=0, keepdims=True)
        pos = jnp.min(jnp.where(s == m, iota, float(n)), axis=0, keepdims=True)
        hit = iota == pos
        vals.append(m)
        if payload is None:
            picks.append(pos)
        else:
            picks.append(jnp.max(jnp.where(hit, payload, -1.0), axis=0, keepdims=True))
        s = jnp.where(hit, -jnp.inf, s)
    return jnp.concatenate(vals, axis=0), jnp.concatenate(picks, axis=0)


def _candidate_grid(s1, i1, s2, i2):
    k = PEER_TOPK
    row8 = lax.broadcasted_iota(jnp.int32, (8, s1.shape[1]), 0)
    vals = [s1[0:1] + s2, s1[1:2] + s2[0:8]]
    idxs = [i1[0:1] * PEER_NKEYS + i2, i1[1:2] * PEER_NKEYS + i2[0:8]]
    for i in range(2, 8):
        ok = row8 < k // (i + 1)
        vals.append(jnp.where(ok, s1[i:i + 1] + s2[0:8], -jnp.inf))
        idxs.append(i1[i:i + 1] * PEER_NKEYS + i2[0:8])
    vals.append(s1[8:16] + s2[0:1])
    idxs.append(i1[8:16] * PEER_NKEYS + i2[0:1])
    return jnp.concatenate(vals, axis=0), jnp.concatenate(idxs, axis=0)


def _topk_kernel(q_ref, k_hi_ref, k_lo_ref, idx_ref, g_ref):
    nt = (((1,), (1,)), ((), ()))
    idx_rows, gate_rows = [], []
    for h in range(PEER_HEADS):
        tops = []
        for p in range(2):
            hp = 2 * h + p
            qs = q_ref[:, hp * PEER_KEY_HALF:(hp + 1) * PEER_KEY_HALF]
            q_hi, q_lo = _split_bf16(qs)
            k_hi = k_hi_ref[hp]
            k_lo = k_lo_ref[hp]
            dg = lambda a, b: lax.dot_general(a, b, nt, preferred_element_type=F32)
            s = dg(k_hi, q_hi) + (dg(k_hi, q_lo) + dg(k_lo, q_hi))
            tops.append(_top16(s))
        (s1, i1), (s2, i2) = tops
        cand, cidx = _candidate_grid(s1, i1, s2, i2)
        sc, idx = _top16(cand, cidx)
        e = jnp.exp(sc - sc[0:1])
        g = e / jnp.sum(e, axis=0, keepdims=True)
        idx_rows.append(idx)
        gate_rows.append(g)
    rows = jnp.concatenate(idx_rows, axis=0) * float(HALF_ROWS) + float(TAB_PAD)
    idx_ref[...] = rows.T.astype(jnp.int32)
    g_ref[...] = jnp.concatenate(gate_rows, axis=0).T


def _peer_topk(q, k_hi, k_lo, tm):
    t, nq = q.shape
    nsel = PEER_NSEL
    full = pl.BlockSpec(k_hi.shape, lambda i: (0, 0, 0))
    out = pl.BlockSpec((tm, nsel), lambda i: (i, 0))
    return pl.pallas_call(
        _topk_kernel,
        grid=(t // tm,),
        in_specs=[pl.BlockSpec((tm, nq), lambda i: (i, 0)), full, full],
        out_specs=[out, out],
        out_shape=[jax.ShapeDtypeStruct((t, nsel), jnp.int32), jax.ShapeDtypeStruct((t, nsel), F32)],
        compiler_params=_params("parallel"),
        name="peer_topk",
    )(q, k_hi, k_lo)


def _pack_table(w):
    e, d = w.shape
    bits = lax.bitcast_convert_type(w.astype(BF16), jnp.uint16).astype(jnp.uint32)
    words = bits[:, :d // 2] | (bits[:, d // 2:] << 16)
    return jnp.pad(words.reshape(e * HALF_ROWS, 128), ((TAB_PAD, TAB_PAD), (0, 0)))


def _halves(words):
    lo = pltpu.bitcast(words << 16, F32)
    hi = pltpu.bitcast(words & jnp.uint32(0xFFFF0000), F32)
    return lo, hi


def _slab_pair(row_ref, t):
    def tile(base):
        rows = [row_ref[t:t + 1, base + c * 128:base + (c + 1) * 128] for c in range(HALF_ROWS)]
        return jnp.concatenate(rows + rows, axis=0)
    return tile(0), tile(D_MODEL // 2)


def _pair_words(tab, row_a, row_b):
    sub = lax.broadcasted_iota(jnp.int32, (8, 128), 0)
    wa = tab[pl.ds(pl.multiple_of(row_a, HALF_ROWS), 8), :]
    wb = tab[pl.ds(pl.multiple_of(row_b - HALF_ROWS, HALF_ROWS), 8), :]
    return jnp.where(sub < HALF_ROWS, wa, wb)


def _half_sums(zs):
    sub = lax.broadcasted_iota(jnp.int32, (8, 128), 0)
    z = [zs[0], zs[2], zs[1], zs[3]]
    for d in (2, 1):
        low = (sub & d) == 0
        z = [jnp.where(low, a + pltpu.roll(a, 8 - d, axis=0), b + pltpu.roll(b, d, axis=0))
             for a, b in zip(z[0::2], z[1::2])]
    return z[0]


def _half_spread(x):
    sub = lax.broadcasted_iota(jnp.int32, (8, 128), 0)
    even = (sub & 1) == 0
    lowp = (sub & 2) == 0
    ev = jnp.where(even, x, pltpu.roll(x, 1, axis=0))
    od = jnp.where(even, pltpu.roll(x, 7, axis=0), x)
    out = []
    for y in (ev, od):
        out.append((jnp.where(lowp, y, pltpu.roll(y, 2, axis=0)),
                    jnp.where(lowp, pltpu.roll(y, 6, axis=0), y)))
    return [out[0][0], out[1][0], out[0][1], out[1][1]]


def _load_table(tab_hbm, tab, sem):
    cp = pltpu.make_async_copy(tab_hbm, tab, sem.at[0])
    cp.start()
    cp.wait()


def _row_blocks(rows_hbm, ibufs, isem, body):
    j = pl.program_id(0)
    nblk = 2 * pl.num_programs(0)

    def rows_copy(blk, s):
        return pltpu.make_async_copy(rows_hbm.at[blk], ibufs[s], isem.at[s])

    @pl.when(j == 0)
    def _():
        rows_copy(0, 0).start()

    for s in range(2):
        blk = 2 * j + s
        rows_copy(blk, s).wait()

        @pl.when(blk + 1 < nblk)
        def _():
            rows_copy(blk + 1, 1 - s).start()

        body(s, ibufs[s])


def _row_scratch():
    n = PEER_TOK * PEER_NSEL
    return [pltpu.SMEM((1, n), jnp.int32), pltpu.SMEM((1, n), jnp.int32), pltpu.SemaphoreType.DMA((2,))]


def _act_kernel(rows_hbm, tab_hbm, h2_ref, gate_ref, w_ref, tab, tsem, ibuf0, ibuf1, isem):
    @pl.when(pl.program_id(0) == 0)
    def _():
        _load_table(tab_hbm, tab, tsem)

    def body(s, idx_ref):
        cols = []
        for t in range(PEER_TOK):
            xa, xb = _slab_pair(h2_ref, s * PEER_TOK + t)
            sums = []
            for g in range(PEER_NSEL // 8):
                zs = []
                for j in range(4):
                    base = t * PEER_NSEL + g * 8 + j
                    lo, hi = _halves(_pair_words(tab, idx_ref[0, base], idx_ref[0, base + 4]))
                    zs.append(lo * xa + hi * xb)
                sums.append(_half_sums(zs))
            cols.append(jnp.sum(jnp.concatenate(sums, axis=0), axis=-1, keepdims=True))
        act = jnp.concatenate(cols, axis=1).T
        tok = slice(s * PEER_TOK, (s + 1) * PEER_TOK)
        w_ref[tok, :] = gate_ref[tok, :] * jax.nn.gelu(act)

    _row_blocks(rows_hbm, (ibuf0, ibuf1), isem, body)


def _peer_act(rows, tab_u, h2, gate):
    t, d = h2.shape
    nb = t // PEER_TOK
    n = PEER_TOK * PEER_NSEL
    sel = pl.BlockSpec((2 * PEER_TOK, PEER_NSEL), lambda i: (i, 0))
    return pl.pallas_call(
        _act_kernel,
        grid=(nb // 2,),
        in_specs=[pl.BlockSpec(memory_space=pl.ANY),
                  pl.BlockSpec(memory_space=pl.ANY),
                  pl.BlockSpec((2 * PEER_TOK, d), lambda i: (i, 0)),
                  sel],
        out_specs=sel,
        out_shape=jax.ShapeDtypeStruct((t, PEER_NSEL), F32),
        scratch_shapes=[pltpu.VMEM(tab_u.shape, jnp.uint32), pltpu.SemaphoreType.DMA((1,))] + _row_scratch(),
        compiler_params=pltpu.CompilerParams(dimension_semantics=("arbitrary",), vmem_limit_bytes=PEER_VMEM),
        name="peer_act",
    )(rows.reshape(nb, 1, n), tab_u, h2, gate)


def _out_kernel(rows_hbm, tab_hbm, w_ref, x1_ref, g2_ref, fg_ref, o_ref, tab, tsem, ibuf0, ibuf1, isem):
    @pl.when(pl.program_id(0) == 0)
    def _():
        _load_table(tab_hbm, tab, tsem)

    g2 = g2_ref[0]
    fg = fg_ref[...]

    def body(s, idx_ref):
        tok = slice(s * PEER_TOK, (s + 1) * PEER_TOK)
        wcol = w_ref[tok, :].T
        outs = []
        for t in range(PEER_TOK):
            wb = jnp.broadcast_to(wcol[:, t:t + 1], (PEER_NSEL, 128))
            acc = [[None, None] for _ in range(4)]
            for g in range(PEER_NSEL // 8):
                spread = _half_spread(wb[g * 8:(g + 1) * 8, :])
                for j in range(4):
                    base = t * PEER_NSEL + g * 8 + j
                    lo, hi = _halves(_pair_words(tab, idx_ref[0, base], idx_ref[0, base + 4]))
                    a = acc[j]
                    a[0] = spread[j] * lo if a[0] is None else a[0] + spread[j] * lo
                    a[1] = spread[j] * hi if a[1] is None else a[1] + spread[j] * hi
            lo = (acc[0][0] + acc[1][0]) + (acc[2][0] + acc[3][0])
            hi = (acc[0][1] + acc[1][1]) + (acc[2][1] + acc[3][1])
            lo = lo + pltpu.roll(lo, HALF_ROWS, axis=0)
            hi = hi + pltpu.roll(hi, HALF_ROWS, axis=0)
            outs.append(jnp.concatenate([lo[c:c + 1] for c in range(HALF_ROWS)] +
                                        [hi[c:c + 1] for c in range(HALF_ROWS)], axis=1))
        y = jnp.concatenate(outs, axis=0)
        o_ref[tok, :] = _rmsnorm(x1_ref[tok, :] + g2 * y, fg)

    _row_blocks(rows_hbm, (ibuf0, ibuf1), isem, body)


def _peer_out(rows, w, tab_v, x1, g2, fg, seq_len):
    t, d = x1.shape
    nb = t // PEER_TOK
    n = PEER_TOK * PEER_NSEL
    tok = pl.BlockSpec((2 * PEER_TOK, d), lambda i: (i, 0))
    return pl.pallas_call(
        _out_kernel,
        grid=(nb // 2,),
        in_specs=[pl.BlockSpec(memory_space=pl.ANY),
                  pl.BlockSpec(memory_space=pl.ANY),
                  pl.BlockSpec((2 * PEER_TOK, PEER_NSEL), lambda i: (i, 0)),
                  tok,
                  pl.BlockSpec((1, 1, d), lambda i: (i * 2 * PEER_TOK // seq_len, 0, 0)),
                  pl.BlockSpec((1, d), lambda i: (0, 0))],
        out_specs=tok,
        out_shape=jax.ShapeDtypeStruct((t, d), F32),
        scratch_shapes=[pltpu.VMEM(tab_v.shape, jnp.uint32), pltpu.SemaphoreType.DMA((1,))] + _row_scratch(),
        compiler_params=pltpu.CompilerParams(dimension_semantics=("arbitrary",), vmem_limit_bytes=PEER_VMEM),
        name="peer_out",
    )(rows.reshape(nb, 1, n), tab_v, w, x1, g2, fg)


def _block_diag(w):
    h, dh, _ = w.shape
    eye = jnp.eye(h, dtype=w.dtype)
    return (eye[:, None, :, None] * w[:, :, None, :]).reshape(h * dh, h * dh)


def kernel(x, c, ctx, c_ctx, w_mod, b_mod, norm1_g, w_in, conv_w, conv_b, lru_w_a, lru_b_a, lru_w_x,
           lru_b_x, lru_lambda, fourier_out_g, lru_out_g, w_out, norm2_g, peer_w_q, peer_sub_keys,
           peer_u, peer_v, final_norm_g):
    b, l, d = x.shape
    depth = w_mod.shape[0]
    assert depth == 1, "context stream update between layers is not implemented"
    assert d == D_MODEL and l % 1024 == 0 and b <= 7
    lyr = 0
    c_len = ctx.shape[1]

    cvec = jnp.zeros((8, d), F32).at[:b].set(c).at[b].set(c_ctx)
    mod = _mod(cvec, w_mod[lyr], b_mod[lyr])
    mod_x = mod[:b].reshape(b, N_MOD, 1, d)
    sh1, sc1, g1, sh2, sc2, g2 = [mod_x[:, k] for k in range(N_MOD)]
    mod_c = jnp.broadcast_to(mod[b].reshape(1, N_MOD, 1, d), (b, N_MOD, 1, d))

    n1g = norm1_g[lyr].reshape(1, d)
    w_in_b = w_in[lyr].astype(BF16)
    fx, ux, gx = _in_proj(x, sh1, sc1, n1g, w_in_b, tm=512)
    _, uc, _ = _in_proj(ctx, mod_c[:, 0], mod_c[:, 1], n1g, w_in_b, tm=c_len)

    wa = jnp.stack([_block_diag(lru_w_a[lyr, k]) for k in range(2)]).astype(BF16)
    wx = jnp.stack([_block_diag(lru_w_x[lyr, k]) for k in range(2)]).astype(BF16)
    lru_args = (conv_w[lyr], conv_b[lyr], wa, lru_b_a[lyr], wx, lru_b_x[lyr], lru_lambda[lyr])
    h_zero = jnp.zeros((b, 2, LRU_WIDTH), F32)
    _, _, fin = _lru(uc, *lru_args, h_zero, row_len=c_len, t=c_len)
    hf, hb, _ = _lru(ux, *lru_args, fin, row_len=GRID_W, t=256)

    fm = _fourier(fx, fourier_out_g[lyr])

    w_out_b = w_out[lyr].astype(BF16)
    wq_hi, wq_lo = _split_bf16(peer_w_q[lyr])
    x1, h2, q = _mix_out(hf, hb, gx, fm, x, g1, sh2, sc2, lru_out_g[lyr].reshape(1, LRU_WIDTH),
                         w_out_b[:FOURIER_WIDTH], w_out_b[FOURIER_WIDTH:], norm2_g[lyr].reshape(1, d),
                         wq_hi, wq_lo, tm=512)

    t = b * l
    keys = peer_sub_keys[lyr].reshape(2 * PEER_HEADS, PEER_NKEYS, PEER_KEY_HALF)
    k_hi, k_lo = _split_bf16(keys)
    rows, gate = _peer_topk(q.reshape(t, -1), k_hi, k_lo, tm=256)
    w = _peer_act(rows, _pack_table(peer_u[lyr]), h2.reshape(t, d), gate)
    out = _peer_out(rows, w, _pack_table(peer_v[lyr]), x1.reshape(t, d), g2,
                    final_norm_g.reshape(1, d), seq_len=l)
    return out.reshape(b, l, d)
```

```python
import functools

import jax
import jax.numpy as jnp
import numpy as np
from jax import lax
from jax.experimental import pallas as pl
from jax.experimental.pallas import tpu as pltpu

F32 = jnp.float32
BF16 = jnp.bfloat16

D_MODEL = 1024
GRID_W = 64
FOURIER_WIDTH = 512
FOURIER_HEADS = 8
FOURIER_HEAD_DIM = FOURIER_WIDTH // FOURIER_HEADS
LRU_WIDTH = 512
LRU_HEADS = 8
IN_WIDTH = FOURIER_WIDTH + 2 * LRU_WIDTH
CONV_W = 4
LRU_C = 8.0
PEER_HEADS = 8
PEER_NKEYS = 128
PEER_KEY_HALF = 128
PEER_TOPK = 16
PEER_NSEL = PEER_HEADS * PEER_TOPK
N_MOD = 6
EPS = 1e-6

HALF_ROWS = D_MODEL // 2 // 128
TAB_PAD = 8
PEER_TOK = 16
PEER_VMEM = 52 * 1024 * 1024

VMEM_LIMIT = 48 * 1024 * 1024


def _params(*sem):
    return pltpu.CompilerParams(dimension_semantics=sem, vmem_limit_bytes=VMEM_LIMIT)


def _split_bf16(a):
    hi = a.astype(BF16)
    lo = (a - hi.astype(F32)).astype(BF16)
    return hi, lo


def _dot(a, b):
    return jnp.dot(a, b, preferred_element_type=F32)


def _dot3(a, b_hi, b_lo):
    a_hi, a_lo = _split_bf16(a)
    return _dot(a_hi, b_hi) + (_dot(a_hi, b_lo) + _dot(a_lo, b_hi))


def _rmsnorm(x, g):
    return x * lax.rsqrt(jnp.mean(x * x, axis=-1, keepdims=True) + EPS) * g


def _mod_kernel(c_ref, w_ref, b_ref, o_ref):
    c = c_ref[...]
    s = c * jax.nn.sigmoid(c)
    o_ref[...] = jnp.dot(s, w_ref[...], preferred_element_type=F32,
                         precision=lax.Precision.HIGHEST) + b_ref[...]


def _mod(cvec, w_mod, b_mod):
    n = w_mod.shape[1]
    bn = 768
    return pl.pallas_call(
        _mod_kernel,
        grid=(n // bn,),
        in_specs=[pl.BlockSpec((8, D_MODEL), lambda j: (0, 0)),
                  pl.BlockSpec((D_MODEL, bn), lambda j: (0, j)),
                  pl.BlockSpec((1, bn), lambda j: (0, j))],
        out_specs=pl.BlockSpec((8, bn), lambda j: (0, j)),
        out_shape=jax.ShapeDtypeStruct((8, n), F32),
        compiler_params=_params("arbitrary"),
        name="mod",
    )(cvec, w_mod, b_mod.reshape(1, n))


def _inproj_kernel(x_ref, sh_ref, sc_ref, g_ref, w_ref, f_ref, u_ref, gg_ref):
    h = _rmsnorm(x_ref[0], g_ref[...]) * (1.0 + sc_ref[0]) + sh_ref[0]
    o = _dot(h.astype(BF16), w_ref[...])
    f_ref[0] = o[:, :FOURIER_WIDTH]
    u_ref[0] = o[:, FOURIER_WIDTH:FOURIER_WIDTH + LRU_WIDTH]
    gg_ref[0] = o[:, FOURIER_WIDTH + LRU_WIDTH:]


def _in_proj(x, sh, sc, g, w_bf16, tm):
    b, l, _ = x.shape
    row = pl.BlockSpec((1, 1, D_MODEL), lambda i, j: (i, 0, 0))
    out = pl.BlockSpec((1, tm, FOURIER_WIDTH), lambda i, j: (i, j, 0))
    shp = jax.ShapeDtypeStruct((b, l, FOURIER_WIDTH), F32)
    return pl.pallas_call(
        _inproj_kernel,
        grid=(b, l // tm),
        in_specs=[pl.BlockSpec((1, tm, D_MODEL), lambda i, j: (i, j, 0)), row, row,
                  pl.BlockSpec((1, D_MODEL), lambda i, j: (0, 0)),
                  pl.BlockSpec((D_MODEL, IN_WIDTH), lambda i, j: (0, 0))],
        out_specs=[out, out, out],
        out_shape=[shp, shp, shp],
        compiler_params=_params("parallel", "arbitrary"),
        name="in_proj",
    )(x, sh, sc, g, w_bf16)


def _conv_rows(x, w, b, row_len):
    t = x.shape[0]
    col = lax.broadcasted_iota(jnp.int32, x.shape, 0) % row_len
    y = b + x * w[CONV_W // 2:CONV_W // 2 + 1]
    for k in range(CONV_W):
        off = k - CONV_W // 2
        if off == 0:
            continue
        shifted = pltpu.roll(x, (-off) % t, axis=0)
        ok = (col + off >= 0) & (col + off < row_len)
        y = y + jnp.where(ok, shifted, 0.0) * w[k:k + 1]
    return y


def _lru_terms(xc, wa, ba, wx, bx, sp):
    xb = xc.astype(BF16)
    r = jax.nn.sigmoid(_dot(xb, wa) + ba)
    i = jax.nn.sigmoid(_dot(xb, wx) + bx)
    log_a = (-LRU_C) * r * sp
    a = jnp.exp(log_a)
    th = jnp.tanh(log_a)
    one_minus_a2 = (-2.0) * th / (1.0 - th)
    return a, jnp.sqrt(one_minus_a2) * (i * xc)


def _scan_block(a, bv, h0, reverse):
    t = a.shape[0]
    row = lax.broadcasted_iota(jnp.int32, a.shape, 0)
    d = 1
    while d < t:
        if reverse:
            ok = row < t - d
            shift = t - d
        else:
            ok = row >= d
            shift = d
        a_s = jnp.where(ok, pltpu.roll(a, shift, axis=0), 1.0)
        b_s = jnp.where(ok, pltpu.roll(bv, shift, axis=0), 0.0)
        bv = a * b_s + bv
        a = a * a_s
        d *= 2
    return a * h0 + bv


def _lru_kernel(uf_ref, ub_ref, cw_ref, cb_ref, wa_ref, ba_ref, wx_ref, bx_ref, lam_ref, h0_ref,
                hf_ref, hb_ref, fin_ref, carry, *, row_len):
    j = pl.program_id(1)

    @pl.when(j == 0)
    def _():
        carry[...] = h0_ref[0]

    cw = cw_ref[...]
    cb = cb_ref[...]
    t = uf_ref.shape[1]
    lam = lam_ref[...]
    sp = jax.nn.softplus(-lam)
    for d, (u_ref, o_ref) in enumerate(((uf_ref, hf_ref), (ub_ref, hb_ref))):
        xc = _conv_rows(u_ref[0], cw, cb, row_len)
        a, bv = _lru_terms(xc, wa_ref[d], ba_ref[d:d + 1], wx_ref[d], bx_ref[d:d + 1], sp[d:d + 1])
        h = _scan_block(a, bv, carry[d:d + 1], reverse=(d == 1))
        o_ref[0] = h
        last = h[0:1] if d == 1 else h[t - 1:t]
        carry[d:d + 1] = last
    fin_ref[0] = carry[...]


def _lru(u, conv_w, conv_b, wa_bd, ba, wx_bd, bx, lam, h0, row_len, t):
    b, l, c = u.shape
    nb = l // t
    kern = functools.partial(_lru_kernel, row_len=row_len)
    full = lambda shape: pl.BlockSpec(shape, lambda i, j: (0,) * len(shape))
    blk_f = pl.BlockSpec((1, t, c), lambda i, j: (i, j, 0))
    blk_b = pl.BlockSpec((1, t, c), lambda i, j: (i, nb - 1 - j, 0))
    st = pl.BlockSpec((1, 2, c), lambda i, j: (i, 0, 0))
    return pl.pallas_call(
        kern,
        grid=(b, nb),
        in_specs=[blk_f, blk_b, full((CONV_W, c)), full((1, c)), full((2, c, c)), full((2, c)),
                  full((2, c, c)), full((2, c)), full((2, c)), st],
        out_specs=[blk_f, blk_b, st],
        out_shape=[jax.ShapeDtypeStruct((b, l, c), F32), jax.ShapeDtypeStruct((b, l, c), F32),
                   jax.ShapeDtypeStruct((b, 2, c), F32)],
        scratch_shapes=[pltpu.VMEM((2, c), F32)],
        compiler_params=_params("parallel", "arbitrary"),
        name="lru",
    )(u, u, conv_w, conv_b.reshape(1, c), wa_bd, ba, wx_bd, bx, lam, h0)


def _dft_rows_kernel(f_ref, c_ref, s_ref, ar_ref, ai_ref):
    x = f_ref[0].astype(BF16)
    ar_ref[0] = _dot(c_ref[...], x)
    ai_ref[0] = -_dot(s_ref[...], x)


def _dft_cols_kernel(ar_ref, ai_ref, tr_ref, ti_ref, c_ref, s_ref, bc_ref, bs_ref, g_ref, o_ref):
    ar = ar_ref[0, 0]
    ai = ai_ref[0, 0]
    tr = tr_ref[0]
    ti = ti_ref[0]
    br = (ar * tr - ai * ti).astype(BF16)
    bi = (ar * ti + ai * tr).astype(BF16)
    c = c_ref[...]
    s = s_ref[...]
    zr = _dot(c, br) + _dot(s, bi)
    zi = _dot(c, bi) - _dot(s, br)
    y = _dot(zr.astype(BF16), bc_ref[...]) + _dot(zi.astype(BF16), bs_ref[...])
    o_ref[0] = _rmsnorm(y, g_ref[...]).astype(o_ref.dtype)


def _dft_mats(n):
    k = np.arange(n)
    ang = 2.0 * np.pi * ((k[:, None] * k[None, :]) % n) / n
    return np.cos(ang), np.sin(ang)


def _fourier(f, g):
    b, l, w = f.shape
    n2 = 128
    n1 = l // n2
    scale = 1.0 / np.sqrt(float(l) * FOURIER_HEAD_DIM)
    c1, s1 = _dft_mats(n1)
    c2, s2 = _dft_mats(n2)
    cc, sc = _dft_mats(FOURIER_HEAD_DIM)
    eye = np.eye(FOURIER_HEADS)
    bc = np.kron(eye, cc) * scale
    bs = np.kron(eye, sc) * scale
    ang = 2.0 * np.pi * ((np.arange(n1)[:, None] * np.arange(n2)[None, :]) % l) / l
    tr = jnp.asarray(np.cos(ang)[:, :, None], F32)
    ti = jnp.asarray(-np.sin(ang)[:, :, None], F32)
    as_bf = lambda m: jnp.asarray(m, BF16)

    cb = min(8192, n2 * w)
    f2 = f.reshape(b, n1, n2 * w)
    blk = pl.BlockSpec((1, n1, cb), lambda i, j: (i, 0, j))
    mat1 = pl.BlockSpec((n1, n1), lambda i, j: (0, 0))
    shp = jax.ShapeDtypeStruct((b, n1, n2 * w), F32)
    ar, ai = pl.pallas_call(
        _dft_rows_kernel,
        grid=(b, n2 * w // cb),
        in_specs=[blk, mat1, mat1],
        out_specs=[blk, blk],
        out_shape=[shp, shp],
        compiler_params=_params("parallel", "arbitrary"),
        name="dft_rows",
    )(f2, as_bf(c1), as_bf(s1))

    ar = ar.reshape(b, n1, n2, w)
    ai = ai.reshape(b, n1, n2, w)
    a_blk = pl.BlockSpec((1, 1, n2, w), lambda i, j: (i, j, 0, 0))
    t_blk = pl.BlockSpec((1, n2, 1), lambda i, j: (j, 0, 0))
    mat2 = pl.BlockSpec((n2, n2), lambda i, j: (0, 0))
    matw = pl.BlockSpec((w, w), lambda i, j: (0, 0))
    out = pl.pallas_call(
        _dft_cols_kernel,
        grid=(b, n1),
        in_specs=[a_blk, a_blk, t_blk, t_blk, mat2, mat2, matw, matw,
                  pl.BlockSpec((1, w), lambda i, j: (0, 0))],
        out_specs=pl.BlockSpec((1, n2, w), lambda i, j: (i, 0, j)),
        out_shape=jax.ShapeDtypeStruct((b, n2, n1 * w), BF16),
        compiler_params=_params("parallel", "arbitrary"),
        name="dft_cols",
    )(ar, ai, tr, ti, as_bf(c2), as_bf(s2), as_bf(bc), as_bf(bs), g.reshape(1, w))
    return out.reshape(b, l, w)


def _mixout_kernel(hf_ref, hb_ref, gg_ref, fm_ref, x_ref, g1_ref, sh2_ref, sc2_ref, lg_ref,
                   wo_f_ref, wo_r_ref, n2g_ref, wq_hi_ref, wq_lo_ref, x1_ref, h2_ref, q_ref):
    rx = (hf_ref[0] + hb_ref[0]) * jax.nn.gelu(gg_ref[0])
    rxn = _rmsnorm(rx, lg_ref[...])
    mx = _dot(fm_ref[0], wo_f_ref[...]) + _dot(rxn.astype(BF16), wo_r_ref[...])
    x1 = x_ref[0] + g1_ref[0] * mx
    x1_ref[0] = x1
    h2 = _rmsnorm(x1, n2g_ref[...]) * (1.0 + sc2_ref[0]) + sh2_ref[0]
    h2_ref[0] = h2
    q_ref[0] = _dot3(h2, wq_hi_ref[...], wq_lo_ref[...])


def _mix_out(hf, hb, gg, fm, x, g1, sh2, sc2, lru_g, wo_f, wo_r, n2g, wq_hi, wq_lo, tm):
    b, l, d = x.shape
    nq = wq_hi.shape[1]
    half = pl.BlockSpec((1, tm, LRU_WIDTH), lambda i, j: (i, j, 0))
    tok = pl.BlockSpec((1, tm, d), lambda i, j: (i, j, 0))
    row = pl.BlockSpec((1, 1, d), lambda i, j: (i, 0, 0))
    full = lambda shape: pl.BlockSpec(shape, lambda i, j: (0,) * len(shape))
    return pl.pallas_call(
        _mixout_kernel,
        grid=(b, l // tm),
        in_specs=[half, half, half, half, tok, row, row, row, full((1, LRU_WIDTH)),
                  full((FOURIER_WIDTH, d)), full((LRU_WIDTH, d)), full((1, d)),
                  full((d, nq)), full((d, nq))],
        out_specs=[tok, tok, pl.BlockSpec((1, tm, nq), lambda i, j: (i, j, 0))],
        out_shape=[jax.ShapeDtypeStruct((b, l, d), F32), jax.ShapeDtypeStruct((b, l, d), F32),
                   jax.ShapeDtypeStruct((b, l, nq), F32)],
        compiler_params=_params("parallel", "arbitrary"),
        name="mix_out",
    )(hf, hb, gg, fm, x, g1, sh2, sc2, lru_g, wo_f, wo_r, n2g, wq_hi, wq_lo)


def _top16(s, payload=None):
    n = s.shape[0]
    iota = lax.broadcasted_iota(jnp.int32, s.shape, 0).astype(F32)
    vals, picks = [], []
    for _ in range(PEER_TOPK):
        m = jnp.max(s, axis=0, keepdims=True)
        pos = jnp.min(jnp.where(s == m, iota, float(n)), axis=0, keepdims=True)
        hit = iota == pos
        vals.append(m)
        if payload is None:
            picks.append(pos)
        else:
            picks.append(jnp.max(jnp.where(hit, payload, -1.0), axis=0, keepdims=True))
        s = jnp.where(hit, -jnp.inf, s)
    return jnp.concatenate(vals, axis=0), jnp.concatenate(picks, axis=0)


def _candidate_grid(s1, i1, s2, i2):
    k = PEER_TOPK
    row8 = lax.broadcasted_iota(jnp.int32, (8, s1.shape[1]), 0)
    vals = [s1[0:1] + s2, s1[1:2] + s2[0:8]]
    idxs = [i1[0:1] * PEER_NKEYS + i2, i1[1:2] * PEER_NKEYS + i2[0:8]]
    for i in range(2, 8):
        ok = row8 < k // (i + 1)
        vals.append(jnp.where(ok, s1[i:i + 1] + s2[0:8], -jnp.inf))
        idxs.append(i1[i:i + 1] * PEER_NKEYS + i2[0:8])
    vals.append(s1[8:16] + s2[0:1])
    idxs.append(i1[8:16] * PEER_NKEYS + i2[0:1])
    return jnp.concatenate(vals, axis=0), jnp.concatenate(idxs, axis=0)


def _topk_kernel(q_ref, k_hi_ref, k_lo_ref, idx_ref, g_ref):
    nt = (((1,), (1,)), ((), ()))
    idx_rows, gate_rows = [], []
    for h in range(PEER_HEADS):
        tops = []
        for p in range(2):
            hp = 2 * h + p
            qs = q_ref[:, hp * PEER_KEY_HALF:(hp + 1) * PEER_KEY_HALF]
            q_hi, q_lo = _split_bf16(qs)
            k_hi = k_hi_ref[hp]
            k_lo = k_lo_ref[hp]
            dg = lambda a, b: lax.dot_general(a, b, nt, preferred_element_type=F32)
            s = dg(k_hi, q_hi) + (dg(k_hi, q_lo) + dg(k_lo, q_hi))
            tops.append(_top16(s))
        (s1, i1), (s2, i2) = tops
        cand, cidx = _candidate_grid(s1, i1, s2, i2)
        sc, idx = _top16(cand, cidx)
        e = jnp.exp(sc - sc[0:1])
        g = e / jnp.sum(e, axis=0, keepdims=True)
        idx_rows.append(idx)
        gate_rows.append(g)
    rows = jnp.concatenate(idx_rows, axis=0) * float(HALF_ROWS) + float(TAB_PAD)
    idx_ref[...] = rows.T.astype(jnp.int32)
    g_ref[...] = jnp.concatenate(gate_rows, axis=0).T


def _peer_topk(q, k_hi, k_lo, tm):
    t, nq = q.shape
    nsel = PEER_NSEL
    full = pl.BlockSpec(k_hi.shape, lambda i: (0, 0, 0))
    out = pl.BlockSpec((tm, nsel), lambda i: (i, 0))
    return pl.pallas_call(
        _topk_kernel,
        grid=(t // tm,),
        in_specs=[pl.BlockSpec((tm, nq), lambda i: (i, 0)), full, full],
        out_specs=[out, out],
        out_shape=[jax.ShapeDtypeStruct((t, nsel), jnp.int32), jax.ShapeDtypeStruct((t, nsel), F32)],
        compiler_params=_params("parallel"),
        name="peer_topk",
    )(q, k_hi, k_lo)


def _pack_table(w):
    e, d = w.shape
    bits = lax.bitcast_convert_type(w.astype(BF16), jnp.uint16).astype(jnp.uint32)
    words = bits[:, :d // 2] | (bits[:, d // 2:] << 16)
    return jnp.pad(words.reshape(e * HALF_ROWS, 128), ((TAB_PAD, TAB_PAD), (0, 0)))


def _halves(words):
    lo = pltpu.bitcast(words << 16, F32)
    hi = pltpu.bitcast(words & jnp.uint32(0xFFFF0000), F32)
    return lo, hi


def _slab_pair(row_ref, t):
    def tile(base):
        rows = [row_ref[t:t + 1, base + c * 128:base + (c + 1) * 128] for c in range(HALF_ROWS)]
        return jnp.concatenate(rows + rows, axis=0)
    return tile(0), tile(D_MODEL // 2)


def _pair_words(tab, row_a, row_b):
    sub = lax.broadcasted_iota(jnp.int32, (8, 128), 0)
    wa = tab[pl.ds(pl.multiple_of(row_a, HALF_ROWS), 8), :]
    wb = tab[pl.ds(pl.multiple_of(row_b - HALF_ROWS, HALF_ROWS), 8), :]
    return jnp.where(sub < HALF_ROWS, wa, wb)


def _half_sums(zs):
    sub = lax.broadcasted_iota(jnp.int32, (8, 128), 0)
    z = [zs[0], zs[2], zs[1], zs[3]]
    for d in (2, 1):
        low = (sub & d) == 0
        z = [jnp.where(low, a + pltpu.roll(a, 8 - d, axis=0), b + pltpu.roll(b, d, axis=0))
             for a, b in zip(z[0::2], z[1::2])]
    return z[0]


def _half_spread(x):
    sub = lax.broadcasted_iota(jnp.int32, (8, 128), 0)
    even = (sub & 1) == 0
    lowp = (sub & 2) == 0
    ev = jnp.where(even, x, pltpu.roll(x, 1, axis=0))
    od = jnp.where(even, pltpu.roll(x, 7, axis=0), x)
    out = []
    for y in (ev, od):
        out.append((jnp.where(lowp, y, pltpu.roll(y, 2, axis=0)),
                    jnp.where(lowp, pltpu.roll(y, 6, axis=0), y)))
    return [out[0][0], out[1][0], out[0][1], out[1][1]]


def _load_table(tab_hbm, tab, sem):
    cp = pltpu.make_async_copy(tab_hbm, tab, sem.at[0])
    cp.start()
    cp.wait()


def _row_blocks(rows_hbm, ibufs, isem, body):
    j = pl.program_id(0)
    nblk = 2 * pl.num_programs(0)

    def rows_copy(blk, s):
        return pltpu.make_async_copy(rows_hbm.at[blk], ibufs[s], isem.at[s])

    @pl.when(j == 0)
    def _():
        rows_copy(0, 0).start()

    for s in range(2):
        blk = 2 * j + s

        @pl.when(blk + 1 < nblk)
        def _():
            rows_copy(blk + 1, 1 - s).start()

        rows_copy(blk, s).wait()
        body(s, ibufs[s])


def _row_scratch():
    n = PEER_TOK * PEER_NSEL
    return [pltpu.SMEM((1, n), jnp.int32), pltpu.SMEM((1, n), jnp.int32), pltpu.SemaphoreType.DMA((2,))]


def _act_kernel(rows_hbm, tab_hbm, h2_ref, gate_ref, w_ref, tab, tsem, ibuf0, ibuf1, isem):
    @pl.when(pl.program_id(0) == 0)
    def _():
        _load_table(tab_hbm, tab, tsem)

    def body(s, idx_ref):
        cols = []
        for t in range(PEER_TOK):
            xa, xb = _slab_pair(h2_ref, s * PEER_TOK + t)
            sums = []
            for g in range(PEER_NSEL // 8):
                zs = []
                for j in range(4):
                    base = t * PEER_NSEL + g * 8 + j
                    lo, hi = _halves(_pair_words(tab, idx_ref[0, base], idx_ref[0, base + 4]))
                    zs.append(lo * xa + hi * xb)
                sums.append(_half_sums(zs))
            cols.append(jnp.sum(jnp.concatenate(sums, axis=0), axis=-1, keepdims=True))
        act = jnp.concatenate(cols, axis=1).T
        tok = slice(s * PEER_TOK, (s + 1) * PEER_TOK)
        w_ref[tok, :] = gate_ref[tok, :] * jax.nn.gelu(act)

    _row_blocks(rows_hbm, (ibuf0, ibuf1), isem, body)


def _peer_act(rows, tab_u, h2, gate):
    t, d = h2.shape
    nb = t // PEER_TOK
    n = PEER_TOK * PEER_NSEL
    sel = pl.BlockSpec((2 * PEER_TOK, PEER_NSEL), lambda i: (i, 0))
    return pl.pallas_call(
        _act_kernel,
        grid=(nb // 2,),
        in_specs=[pl.BlockSpec(memory_space=pl.ANY),
                  pl.BlockSpec(memory_space=pl.ANY),
                  pl.BlockSpec((2 * PEER_TOK, d), lambda i: (i, 0)),
                  sel],
        out_specs=sel,
        out_shape=jax.ShapeDtypeStruct((t, PEER_NSEL), F32),
        scratch_shapes=[pltpu.VMEM(tab_u.shape, jnp.uint32), pltpu.SemaphoreType.DMA((1,))] + _row_scratch(),
        compiler_params=pltpu.CompilerParams(dimension_semantics=("arbitrary",), vmem_limit_bytes=PEER_VMEM),
        name="peer_act",
    )(rows.reshape(nb, 1, n), tab_u, h2, gate)


def _out_kernel(rows_hbm, tab_hbm, w_ref, x1_ref, g2_ref, fg_ref, o_ref, tab, tsem, ibuf0, ibuf1, isem):
    @pl.when(pl.program_id(0) == 0)
    def _():
        _load_table(tab_hbm, tab, tsem)

    g2 = g2_ref[0]
    fg = fg_ref[...]

    def body(s, idx_ref):
        tok = slice(s * PEER_TOK, (s + 1) * PEER_TOK)
        wcol = w_ref[tok, :].T
        outs = []
        for t in range(PEER_TOK):
            wb = jnp.broadcast_to(wcol[:, t:t + 1], (PEER_NSEL, 128))
            acc = [[None, None] for _ in range(4)]
            for g in range(PEER_NSEL // 8):
                spread = _half_spread(wb[g * 8:(g + 1) * 8, :])
                for j in range(4):
                    base = t * PEER_NSEL + g * 8 + j
                    lo, hi = _halves(_pair_words(tab, idx_ref[0, base], idx_ref[0, base + 4]))
                    a = acc[j]
                    a[0] = spread[j] * lo if a[0] is None else a[0] + spread[j] * lo
                    a[1] = spread[j] * hi if a[1] is None else a[1] + spread[j] * hi
            lo = (acc[0][0] + acc[1][0]) + (acc[2][0] + acc[3][0])
            hi = (acc[0][1] + acc[1][1]) + (acc[2][1] + acc[3][1])
            lo = lo + pltpu.roll(lo, HALF_ROWS, axis=0)
            hi = hi + pltpu.roll(hi, HALF_ROWS, axis=0)
            outs.append(jnp.concatenate([lo[c:c + 1] for c in range(HALF_ROWS)] +
                                        [hi[c:c + 1] for c in range(HALF_ROWS)], axis=1))
        y = jnp.concatenate(outs, axis=0)
        o_ref[tok, :] = _rmsnorm(x1_ref[tok, :] + g2 * y, fg)

    _row_blocks(rows_hbm, (ibuf0, ibuf1), isem, body)


def _peer_out(rows, w, tab_v, x1, g2, fg, seq_len):
    t, d = x1.shape
    nb = t // PEER_TOK
    n = PEER_TOK * PEER_NSEL
    tok = pl.BlockSpec((2 * PEER_TOK, d), lambda i: (i, 0))
    return pl.pallas_call(
        _out_kernel,
        grid=(nb // 2,),
        in_specs=[pl.BlockSpec(memory_space=pl.ANY),
                  pl.BlockSpec(memory_space=pl.ANY),
                  pl.BlockSpec((2 * PEER_TOK, PEER_NSEL), lambda i: (i, 0)),
                  tok,
                  pl.BlockSpec((1, 1, d), lambda i: (i * 2 * PEER_TOK // seq_len, 0, 0)),
                  pl.BlockSpec((1, d), lambda i: (0, 0))],
        out_specs=tok,
        out_shape=jax.ShapeDtypeStruct((t, d), F32),
        scratch_shapes=[pltpu.VMEM(tab_v.shape, jnp.uint32), pltpu.SemaphoreType.DMA((1,))] + _row_scratch(),
        compiler_params=pltpu.CompilerParams(dimension_semantics=("arbitrary",), vmem_limit_bytes=PEER_VMEM),
        name="peer_out",
    )(rows.reshape(nb, 1, n), tab_v, w, x1, g2, fg)


def _block_diag(w):
    h, dh, _ = w.shape
    eye = jnp.eye(h, dtype=w.dtype)
    return (eye[:, None, :, None] * w[:, :, None, :]).reshape(h * dh, h * dh)


def kernel(x, c, ctx, c_ctx, w_mod, b_mod, norm1_g, w_in, conv_w, conv_b, lru_w_a, lru_b_a, lru_w_x,
           lru_b_x, lru_lambda, fourier_out_g, lru_out_g, w_out, norm2_g, peer_w_q, peer_sub_keys,
           peer_u, peer_v, final_norm_g):
    b, l, d = x.shape
    depth = w_mod.shape[0]
    assert depth == 1, "context stream update between layers is not implemented"
    assert d == D_MODEL and l % 1024 == 0 and b <= 7
    lyr = 0
    c_len = ctx.shape[1]

    cvec = jnp.zeros((8, d), F32).at[:b].set(c).at[b].set(c_ctx)
    mod = _mod(cvec, w_mod[lyr], b_mod[lyr])
    mod_x = mod[:b].reshape(b, N_MOD, 1, d)
    sh1, sc1, g1, sh2, sc2, g2 = [mod_x[:, k] for k in range(N_MOD)]
    mod_c = jnp.broadcast_to(mod[b].reshape(1, N_MOD, 1, d), (b, N_MOD, 1, d))

    n1g = norm1_g[lyr].reshape(1, d)
    w_in_b = w_in[lyr].astype(BF16)
    fx, ux, gx = _in_proj(x, sh1, sc1, n1g, w_in_b, tm=512)
    _, uc, _ = _in_proj(ctx, mod_c[:, 0], mod_c[:, 1], n1g, w_in_b, tm=c_len)

    wa = jnp.stack([_block_diag(lru_w_a[lyr, k]) for k in range(2)]).astype(BF16)
    wx = jnp.stack([_block_diag(lru_w_x[lyr, k]) for k in range(2)]).astype(BF16)
    lru_args = (conv_w[lyr], conv_b[lyr], wa, lru_b_a[lyr], wx, lru_b_x[lyr], lru_lambda[lyr])
    h_zero = jnp.zeros((b, 2, LRU_WIDTH), F32)
    _, _, fin = _lru(uc, *lru_args, h_zero, row_len=c_len, t=c_len)
    hf, hb, _ = _lru(ux, *lru_args, fin, row_len=GRID_W, t=256)

    fm = _fourier(fx, fourier_out_g[lyr])

    w_out_b = w_out[lyr].astype(BF16)
    wq_hi, wq_lo = _split_bf16(peer_w_q[lyr])
    x1, h2, q = _mix_out(hf, hb, gx, fm, x, g1, sh2, sc2, lru_out_g[lyr].reshape(1, LRU_WIDTH),
                         w_out_b[:FOURIER_WIDTH], w_out_b[FOURIER_WIDTH:], norm2_g[lyr].reshape(1, d),
                         wq_hi, wq_lo, tm=512)

    t = b * l
    keys = peer_sub_keys[lyr].reshape(2 * PEER_HEADS, PEER_NKEYS, PEER_KEY_HALF)
    k_hi, k_lo = _split_bf16(keys)
    rows, gate = _peer_topk(q.reshape(t, -1), k_hi, k_lo, tm=256)
    w = _peer_act(rows, _pack_table(peer_u[lyr]), h2.reshape(t, d), gate)
    out = _peer_out(rows, w, _pack_table(peer_v[lyr]), x1.reshape(t, d), g2,
                    final_norm_g.reshape(1, d), seq_len=l)
    return out.reshape(b, l, d)
```

```python
import functools

import jax
import jax.numpy as jnp
import numpy as np
from jax import lax
from jax.experimental import pallas as pl
from jax.experimental.pallas import tpu as pltpu

F32 = jnp.float32
BF16 = jnp.bfloat16

D_MODEL = 1024
GRID_W = 64
FOURIER_WIDTH = 512
FOURIER_HEADS = 8
FOURIER_HEAD_DIM = FOURIER_WIDTH // FOURIER_HEADS
LRU_WIDTH = 512
LRU_HEADS = 8
IN_WIDTH = FOURIER_WIDTH + 2 * LRU_WIDTH
CONV_W = 4
LRU_C = 8.0
PEER_HEADS = 8
PEER_NKEYS = 128
PEER_KEY_HALF = 128
PEER_TOPK = 16
PEER_NSEL = PEER_HEADS * PEER_TOPK
N_MOD = 6
EPS = 1e-6

HALF_ROWS = D_MODEL // 2 // 128
TAB_PAD = 8
PEER_TOK = 16
PEER_VMEM = 52 * 1024 * 1024

VMEM_LIMIT = 48 * 1024 * 1024


def _params(*sem):
    return pltpu.CompilerParams(dimension_semantics=sem, vmem_limit_bytes=VMEM_LIMIT)


def _split_bf16(a):
    hi = a.astype(BF16)
    lo = (a - hi.astype(F32)).astype(BF16)
    return hi, lo


def _dot(a, b):
    return jnp.dot(a, b, preferred_element_type=F32)


def _dot3(a, b_hi, b_lo):
    a_hi, a_lo = _split_bf16(a)
    return _dot(a_hi, b_hi) + (_dot(a_hi, b_lo) + _dot(a_lo, b_hi))


def _rmsnorm(x, g):
    return x * lax.rsqrt(jnp.mean(x * x, axis=-1, keepdims=True) + EPS) * g


def _mod_kernel(c_ref, w_ref, b_ref, o_ref):
    c = c_ref[...]
    s = c * jax.nn.sigmoid(c)
    o_ref[...] = jnp.dot(s, w_ref[...], preferred_element_type=F32,
                         precision=lax.Precision.HIGHEST) + b_ref[...]


def _mod(cvec, w_mod, b_mod):
    n = w_mod.shape[1]
    bn = 768
    return pl.pallas_call(
        _mod_kernel,
        grid=(n // bn,),
        in_specs=[pl.BlockSpec((8, D_MODEL), lambda j: (0, 0)),
                  pl.BlockSpec((D_MODEL, bn), lambda j: (0, j)),
                  pl.BlockSpec((1, bn), lambda j: (0, j))],
        out_specs=pl.BlockSpec((8, bn), lambda j: (0, j)),
        out_shape=jax.ShapeDtypeStruct((8, n), F32),
        compiler_params=_params("arbitrary"),
        name="mod",
    )(cvec, w_mod, b_mod.reshape(1, n))


def _inproj_kernel(x_ref, sh_ref, sc_ref, g_ref, w_ref, f_ref, u_ref, gg_ref):
    h = _rmsnorm(x_ref[0], g_ref[...]) * (1.0 + sc_ref[0]) + sh_ref[0]
    o = _dot(h.astype(BF16), w_ref[...])
    f_ref[0] = o[:, :FOURIER_WIDTH]
    u_ref[0] = o[:, FOURIER_WIDTH:FOURIER_WIDTH + LRU_WIDTH]
    gg_ref[0] = o[:, FOURIER_WIDTH + LRU_WIDTH:]


def _in_proj(x, sh, sc, g, w_bf16, tm):
    b, l, _ = x.shape
    row = pl.BlockSpec((1, 1, D_MODEL), lambda i, j: (i, 0, 0))
    out = pl.BlockSpec((1, tm, FOURIER_WIDTH), lambda i, j: (i, j, 0))
    shp = jax.ShapeDtypeStruct((b, l, FOURIER_WIDTH), F32)
    return pl.pallas_call(
        _inproj_kernel,
        grid=(b, l // tm),
        in_specs=[pl.BlockSpec((1, tm, D_MODEL), lambda i, j: (i, j, 0)), row, row,
                  pl.BlockSpec((1, D_MODEL), lambda i, j: (0, 0)),
                  pl.BlockSpec((D_MODEL, IN_WIDTH), lambda i, j: (0, 0))],
        out_specs=[out, out, out],
        out_shape=[shp, shp, shp],
        compiler_params=_params("parallel", "arbitrary"),
        name="in_proj",
    )(x, sh, sc, g, w_bf16)


def _conv_rows(x, w, b, row_len):
    t = x.shape[0]
    col = lax.broadcasted_iota(jnp.int32, x.shape, 0) % row_len
    y = b + x * w[CONV_W // 2:CONV_W // 2 + 1]
    for k in range(CONV_W):
        off = k - CONV_W // 2
        if off == 0:
            continue
        shifted = pltpu.roll(x, (-off) % t, axis=0)
        ok = (col + off >= 0) & (col + off < row_len)
        y = y + jnp.where(ok, shifted, 0.0) * w[k:k + 1]
    return y


def _lru_terms(xc, wa, ba, wx, bx, sp):
    xb = xc.astype(BF16)
    r = jax.nn.sigmoid(_dot(xb, wa) + ba)
    i = jax.nn.sigmoid(_dot(xb, wx) + bx)
    log_a = (-LRU_C) * r * sp
    a = jnp.exp(log_a)
    th = jnp.tanh(log_a)
    one_minus_a2 = (-2.0) * th / (1.0 - th)
    return a, jnp.sqrt(one_minus_a2) * (i * xc)


def _scan_block(a, bv, h0, reverse):
    t = a.shape[0]
    row = lax.broadcasted_iota(jnp.int32, a.shape, 0) % 8
    for d in (1, 2, 4):
        if reverse:
            ok = row < 8 - d
            shift = t - d
        else:
            ok = row >= d
            shift = d
        a_s = jnp.where(ok, pltpu.roll(a, shift, axis=0), 1.0)
        b_s = jnp.where(ok, pltpu.roll(bv, shift, axis=0), 0.0)
        bv = a * b_s + bv
        a = a * a_s
    ngroups = t // 8
    hs = [None] * ngroups
    carry = h0
    for g in (reversed(range(ngroups)) if reverse else range(ngroups)):
        hg = a[8 * g:8 * g + 8] * carry + bv[8 * g:8 * g + 8]
        hs[g] = hg
        carry = hg[0:1] if reverse else hg[7:8]
    return jnp.concatenate(hs, axis=0)


def _lru_kernel(uf_ref, ub_ref, cw_ref, cb_ref, wa_ref, ba_ref, wx_ref, bx_ref, lam_ref, h0_ref,
                hf_ref, hb_ref, fin_ref, carry, *, row_len):
    j = pl.program_id(1)

    @pl.when(j == 0)
    def _():
        carry[...] = h0_ref[0]

    cw = cw_ref[...]
    cb = cb_ref[...]
    t = uf_ref.shape[1]
    lam = lam_ref[...]
    sp = jax.nn.softplus(-lam)
    for d, (u_ref, o_ref) in enumerate(((uf_ref, hf_ref), (ub_ref, hb_ref))):
        xc = _conv_rows(u_ref[0], cw, cb, row_len)
        a, bv = _lru_terms(xc, wa_ref[d], ba_ref[d:d + 1], wx_ref[d], bx_ref[d:d + 1], sp[d:d + 1])
        h = _scan_block(a, bv, carry[d:d + 1], reverse=(d == 1))
        o_ref[0] = h
        last = h[0:1] if d == 1 else h[t - 1:t]
        carry[d:d + 1] = last
    fin_ref[0] = carry[...]


def _lru(u, conv_w, conv_b, wa_bd, ba, wx_bd, bx, lam, h0, row_len, t):
    b, l, c = u.shape
    nb = l // t
    kern = functools.partial(_lru_kernel, row_len=row_len)
    full = lambda shape: pl.BlockSpec(shape, lambda i, j: (0,) * len(shape))
    blk_f = pl.BlockSpec((1, t, c), lambda i, j: (i, j, 0))
    blk_b = pl.BlockSpec((1, t, c), lambda i, j: (i, nb - 1 - j, 0))
    st = pl.BlockSpec((1, 2, c), lambda i, j: (i, 0, 0))
    return pl.pallas_call(
        kern,
        grid=(b, nb),
        in_specs=[blk_f, blk_b, full((CONV_W, c)), full((1, c)), full((2, c, c)), full((2, c)),
                  full((2, c, c)), full((2, c)), full((2, c)), st],
        out_specs=[blk_f, blk_b, st],
        out_shape=[jax.ShapeDtypeStruct((b, l, c), F32), jax.ShapeDtypeStruct((b, l, c), F32),
                   jax.ShapeDtypeStruct((b, 2, c), F32)],
        scratch_shapes=[pltpu.VMEM((2, c), F32)],
        compiler_params=_params("parallel", "arbitrary"),
        name="lru",
    )(u, u, conv_w, conv_b.reshape(1, c), wa_bd, ba, wx_bd, bx, lam, h0)


def _dft_rows_kernel(f_ref, c_ref, s_ref, ar_ref, ai_ref):
    x = f_ref[0].astype(BF16)
    ar_ref[0] = _dot(c_ref[...], x)
    ai_ref[0] = -_dot(s_ref[...], x)


def _dft_cols_kernel(ar_ref, ai_ref, tr_ref, ti_ref, c_ref, s_ref, bc_ref, bs_ref, g_ref, o_ref):
    ar = ar_ref[0, 0]
    ai = ai_ref[0, 0]
    tr = tr_ref[0]
    ti = ti_ref[0]
    br = (ar * tr - ai * ti).astype(BF16)
    bi = (ar * ti + ai * tr).astype(BF16)
    c = c_ref[...]
    s = s_ref[...]
    zr = _dot(c, br) + _dot(s, bi)
    zi = _dot(c, bi) - _dot(s, br)
    y = _dot(zr.astype(BF16), bc_ref[...]) + _dot(zi.astype(BF16), bs_ref[...])
    o_ref[0] = _rmsnorm(y, g_ref[...]).astype(o_ref.dtype)


def _dft_mats(n):
    k = np.arange(n)
    ang = 2.0 * np.pi * ((k[:, None] * k[None, :]) % n) / n
    return np.cos(ang), np.sin(ang)


def _fourier(f, g):
    b, l, w = f.shape
    n2 = 128
    n1 = l // n2
    scale = 1.0 / np.sqrt(float(l) * FOURIER_HEAD_DIM)
    c1, s1 = _dft_mats(n1)
    c2, s2 = _dft_mats(n2)
    cc, sc = _dft_mats(FOURIER_HEAD_DIM)
    eye = np.eye(FOURIER_HEADS)
    bc = np.kron(eye, cc) * scale
    bs = np.kron(eye, sc) * scale
    ang = 2.0 * np.pi * ((np.arange(n1)[:, None] * np.arange(n2)[None, :]) % l) / l
    tr = jnp.asarray(np.cos(ang)[:, :, None], F32)
    ti = jnp.asarray(-np.sin(ang)[:, :, None], F32)
    as_bf = lambda m: jnp.asarray(m, BF16)

    cb = min(8192, n2 * w)
    f2 = f.reshape(b, n1, n2 * w)
    blk = pl.BlockSpec((1, n1, cb), lambda i, j: (i, 0, j))
    mat1 = pl.BlockSpec((n1, n1), lambda i, j: (0, 0))
    shp = jax.ShapeDtypeStruct((b, n1, n2 * w), F32)
    ar, ai = pl.pallas_call(
        _dft_rows_kernel,
        grid=(b, n2 * w // cb),
        in_specs=[blk, mat1, mat1],
        out_specs=[blk, blk],
        out_shape=[shp, shp],
        compiler_params=_params("parallel", "arbitrary"),
        name="dft_rows",
    )(f2, as_bf(c1), as_bf(s1))

    ar = ar.reshape(b, n1, n2, w)
    ai = ai.reshape(b, n1, n2, w)
    a_blk = pl.BlockSpec((1, 1, n2, w), lambda i, j: (i, j, 0, 0))
    t_blk = pl.BlockSpec((1, n2, 1), lambda i, j: (j, 0, 0))
    mat2 = pl.BlockSpec((n2, n2), lambda i, j: (0, 0))
    matw = pl.BlockSpec((w, w), lambda i, j: (0, 0))
    out = pl.pallas_call(
        _dft_cols_kernel,
        grid=(b, n1),
        in_specs=[a_blk, a_blk, t_blk, t_blk, mat2, mat2, matw, matw,
                  pl.BlockSpec((1, w), lambda i, j: (0, 0))],
        out_specs=pl.BlockSpec((1, n2, w), lambda i, j: (i, 0, j)),
        out_shape=jax.ShapeDtypeStruct((b, n2, n1 * w), BF16),
        compiler_params=_params("parallel", "arbitrary"),
        name="dft_cols",
    )(ar, ai, tr, ti, as_bf(c2), as_bf(s2), as_bf(bc), as_bf(bs), g.reshape(1, w))
    return out.reshape(b, l, w)


def _mixout_kernel(hf_ref, hb_ref, gg_ref, fm_ref, x_ref, g1_ref, sh2_ref, sc2_ref, lg_ref,
                   wo_f_ref, wo_r_ref, n2g_ref, wq_hi_ref, wq_lo_ref, x1_ref, h2_ref, q_ref):
    rx = (hf_ref[0] + hb_ref[0]) * jax.nn.gelu(gg_ref[0])
    rxn = _rmsnorm(rx, lg_ref[...])
    mx = _dot(fm_ref[0], wo_f_ref[...]) + _dot(rxn.astype(BF16), wo_r_ref[...])
    x1 = x_ref[0] + g1_ref[0] * mx
    x1_ref[0] = x1
    h2 = _rmsnorm(x1, n2g_ref[...]) * (1.0 + sc2_ref[0]) + sh2_ref[0]
    h2_ref[0] = h2
    q_ref[0] = _dot3(h2, wq_hi_ref[...], wq_lo_ref[...])


def _mix_out(hf, hb, gg, fm, x, g1, sh2, sc2, lru_g, wo_f, wo_r, n2g, wq_hi, wq_lo, tm):
    b, l, d = x.shape
    nq = wq_hi.shape[1]
    half = pl.BlockSpec((1, tm, LRU_WIDTH), lambda i, j: (i, j, 0))
    tok = pl.BlockSpec((1, tm, d), lambda i, j: (i, j, 0))
    row = pl.BlockSpec((1, 1, d), lambda i, j: (i, 0, 0))
    full = lambda shape: pl.BlockSpec(shape, lambda i, j: (0,) * len(shape))
    return pl.pallas_call(
        _mixout_kernel,
        grid=(b, l // tm),
        in_specs=[half, half, half, half, tok, row, row, row, full((1, LRU_WIDTH)),
                  full((FOURIER_WIDTH, d)), full((LRU_WIDTH, d)), full((1, d)),
                  full((d, nq)), full((d, nq))],
        out_specs=[tok, tok, pl.BlockSpec((1, tm, nq), lambda i, j: (i, j, 0))],
        out_shape=[jax.ShapeDtypeStruct((b, l, d), F32), jax.ShapeDtypeStruct((b, l, d), F32),
                   jax.ShapeDtypeStruct((b, l, nq), F32)],
        compiler_params=_params("parallel", "arbitrary"),
        name="mix_out",
    )(hf, hb, gg, fm, x, g1, sh2, sc2, lru_g, wo_f, wo_r, n2g, wq_hi, wq_lo)


def _top16(s, payload=None):
    n = s.shape[0]
    iota = lax.broadcasted_iota(jnp.int32, s.shape, 0).astype(F32)
    vals, picks = [], []
    for _ in range(PEER_TOPK):
        m = jnp.max(s, axis=0, keepdims=True)
        pos = jnp.min(jnp.where(s == m, iota, float(n)), axis=0, keepdims=True)
        hit = iota == pos
        vals.append(m)
        if payload is None:
            picks.append(pos)
        else:
            picks.append(jnp.max(jnp.where(hit, payload, -1.0), axis=0, keepdims=True))
        s = jnp.where(hit, -jnp.inf, s)
    return jnp.concatenate(vals, axis=0), jnp.concatenate(picks, axis=0)


def _candidate_grid(s1, i1, s2, i2):
    k = PEER_TOPK
    row8 = lax.broadcasted_iota(jnp.int32, (8, s1.shape[1]), 0)
    vals = [s1[0:1] + s2, s1[1:2] + s2[0:8]]
    idxs = [i1[0:1] * PEER_NKEYS + i2, i1[1:2] * PEER_NKEYS + i2[0:8]]
    for i in range(2, 8):
        ok = row8 < k // (i + 1)
        vals.append(jnp.where(ok, s1[i:i + 1] + s2[0:8], -jnp.inf))
        idxs.append(i1[i:i + 1] * PEER_NKEYS + i2[0:8])
    vals.append(s1[8:16] + s2[0:1])
    idxs.append(i1[8:16] * PEER_NKEYS + i2[0:1])
    return jnp.concatenate(vals, axis=0), jnp.concatenate(idxs, axis=0)


def _topk_kernel(q_ref, k_hi_ref, k_lo_ref, idx_ref, g_ref):
    nt = (((1,), (1,)), ((), ()))
    idx_rows, gate_rows = [], []
    for h in range(PEER_HEADS):
        tops = []
        for p in range(2):
            hp = 2 * h + p
            qs = q_ref[:, hp * PEER_KEY_HALF:(hp + 1) * PEER_KEY_HALF]
            q_hi, q_lo = _split_bf16(qs)
            k_hi = k_hi_ref[hp]
            k_lo = k_lo_ref[hp]
            dg = lambda a, b: lax.dot_general(a, b, nt, preferred_element_type=F32)
            s = dg(k_hi, q_hi) + (dg(k_hi, q_lo) + dg(k_lo, q_hi))
            tops.append(_top16(s))
        (s1, i1), (s2, i2) = tops
        cand, cidx = _candidate_grid(s1, i1, s2, i2)
        sc, idx = _top16(cand, cidx)
        e = jnp.exp(sc - sc[0:1])
        g = e / jnp.sum(e, axis=0, keepdims=True)
        idx_rows.append(idx)
        gate_rows.append(g)
    rows = jnp.concatenate(idx_rows, axis=0) * float(HALF_ROWS) + float(TAB_PAD)
    idx_ref[...] = rows.T.astype(jnp.int32)
    g_ref[...] = jnp.concatenate(gate_rows, axis=0).T


def _peer_topk(q, k_hi, k_lo, tm):
    t, nq = q.shape
    nsel = PEER_NSEL
    full = pl.BlockSpec(k_hi.shape, lambda i: (0, 0, 0))
    out = pl.BlockSpec((tm, nsel), lambda i: (i, 0))
    return pl.pallas_call(
        _topk_kernel,
        grid=(t // tm,),
        in_specs=[pl.BlockSpec((tm, nq), lambda i: (i, 0)), full, full],
        out_specs=[out, out],
        out_shape=[jax.ShapeDtypeStruct((t, nsel), jnp.int32), jax.ShapeDtypeStruct((t, nsel), F32)],
        compiler_params=_params("parallel"),
        name="peer_topk",
    )(q, k_hi, k_lo)


def _pack_table(w):
    e, d = w.shape
    bits = lax.bitcast_convert_type(w.astype(BF16), jnp.uint16).astype(jnp.uint32)
    words = bits[:, :d // 2] | (bits[:, d // 2:] << 16)
    return jnp.pad(words.reshape(e * HALF_ROWS, 128), ((TAB_PAD, TAB_PAD), (0, 0)))


def _halves(words):
    lo = pltpu.bitcast(words << 16, F32)
    hi = pltpu.bitcast(words & jnp.uint32(0xFFFF0000), F32)
    return lo, hi


def _slab_pair(row_ref, t):
    def tile(base):
        rows = [row_ref[t:t + 1, base + c * 128:base + (c + 1) * 128] for c in range(HALF_ROWS)]
        return jnp.concatenate(rows + rows, axis=0)
    return tile(0), tile(D_MODEL // 2)


def _pair_words(tab, row_a, row_b):
    sub = lax.broadcasted_iota(jnp.int32, (8, 128), 0)
    wa = tab[pl.ds(pl.multiple_of(row_a, HALF_ROWS), 8), :]
    wb = tab[pl.ds(pl.multiple_of(row_b - HALF_ROWS, HALF_ROWS), 8), :]
    return jnp.where(sub < HALF_ROWS, wa, wb)


PEER_CHUNK = 32


def _pair_sum_matrix():
    g = np.zeros((PEER_CHUNK, 4 * PEER_CHUNK), np.float32)
    for p in range(PEER_CHUNK):
        g[p, 4 * p:4 * p + 4] = 1.0
    return jnp.asarray(g, BF16)


def _expand_matrix():
    r = np.zeros((PEER_NSEL, 8 * PEER_NSEL), np.float32)
    for p in range(PEER_NSEL):
        r[p, 8 * p:8 * p + 8] = 1.0
    return jnp.asarray(r, BF16)


def _chunk_row_mask():
    q = np.arange(8 * PEER_NSEL)
    s, h = (q % 16) // 2, q % 2
    target = (s % 4) + 4 * h
    return jnp.asarray((target[None, :] == np.arange(8)[:, None]).astype(np.float32), F32)


def _load_table(tab_hbm, tab, sem):
    cp = pltpu.make_async_copy(tab_hbm, tab, sem.at[0])
    cp.start()
    cp.wait()


def _row_blocks(rows_hbm, ibufs, isem, body):
    j = pl.program_id(0)
    nblk = 2 * pl.num_programs(0)

    def rows_copy(blk, s):
        return pltpu.make_async_copy(rows_hbm.at[blk], ibufs[s], isem.at[s])

    @pl.when(j == 0)
    def _():
        rows_copy(0, 0).start()

    for s in range(2):
        blk = 2 * j + s

        @pl.when(blk + 1 < nblk)
        def _():
            rows_copy(blk + 1, 1 - s).start()

        rows_copy(blk, s).wait()
        body(s, ibufs[s])


def _row_scratch():
    n = PEER_TOK * PEER_NSEL
    return [pltpu.SMEM((1, n), jnp.int32), pltpu.SMEM((1, n), jnp.int32), pltpu.SemaphoreType.DMA((2,))]


def _act_kernel(rows_hbm, tab_hbm, h2_ref, gate_ref, gs_ref, w_ref, tab, tsem, ibuf0, ibuf1, isem):
    @pl.when(pl.program_id(0) == 0)
    def _():
        _load_table(tab_hbm, tab, tsem)

    gsum = gs_ref[...]

    def body(s, idx_ref):
        cols = []
        for t in range(PEER_TOK):
            xa, xb = _slab_pair(h2_ref, s * PEER_TOK + t)
            sums = []
            for c in range(PEER_NSEL // PEER_CHUNK):
                prods = []
                for m in range(PEER_CHUNK // 2):
                    base = t * PEER_NSEL + c * PEER_CHUNK + 2 * m
                    lo, hi = _halves(_pair_words(tab, idx_ref[0, base], idx_ref[0, base + 1]))
                    prods.append(lo * xa + hi * xb)
                stack = jnp.concatenate(prods, axis=0).astype(BF16)
                sums.append(_dot(gsum, stack))
            cols.append(jnp.sum(jnp.concatenate(sums, axis=0), axis=-1, keepdims=True))
        act = jnp.concatenate(cols, axis=1).T
        tok = slice(s * PEER_TOK, (s + 1) * PEER_TOK)
        w_ref[tok, :] = gate_ref[tok, :] * jax.nn.gelu(act)

    _row_blocks(rows_hbm, (ibuf0, ibuf1), isem, body)


def _peer_act(rows, tab_u, h2, gate):
    t, d = h2.shape
    nb = t // PEER_TOK
    n = PEER_TOK * PEER_NSEL
    sel = pl.BlockSpec((2 * PEER_TOK, PEER_NSEL), lambda i: (i, 0))
    return pl.pallas_call(
        _act_kernel,
        grid=(nb // 2,),
        in_specs=[pl.BlockSpec(memory_space=pl.ANY),
                  pl.BlockSpec(memory_space=pl.ANY),
                  pl.BlockSpec((2 * PEER_TOK, d), lambda i: (i, 0)),
                  sel,
                  pl.BlockSpec((PEER_CHUNK, 4 * PEER_CHUNK), lambda i: (0, 0))],
        out_specs=sel,
        out_shape=jax.ShapeDtypeStruct((t, PEER_NSEL), F32),
        scratch_shapes=[pltpu.VMEM(tab_u.shape, jnp.uint32), pltpu.SemaphoreType.DMA((1,))] + _row_scratch(),
        compiler_params=pltpu.CompilerParams(dimension_semantics=("arbitrary",), vmem_limit_bytes=PEER_VMEM),
        name="peer_act",
    )(rows.reshape(nb, 1, n), tab_u, h2, gate, _pair_sum_matrix())


def _out_kernel(rows_hbm, tab_hbm, w_ref, x1_ref, g2_ref, fg_ref, ex_ref, mk_ref, o_ref, tab, tsem,
                ibuf0, ibuf1, isem):
    @pl.when(pl.program_id(0) == 0)
    def _():
        _load_table(tab_hbm, tab, tsem)

    g2 = g2_ref[0]
    fg = fg_ref[...]

    expand = ex_ref[...]
    mask = mk_ref[...]

    def body(s, idx_ref):
        tok = slice(s * PEER_TOK, (s + 1) * PEER_TOK)
        w_hi, w_lo = _split_bf16(w_ref[tok, :])
        wide_hi = _dot(w_hi, expand)
        wide_lo = _dot(w_lo, expand)
        outs = []
        for t in range(PEER_TOK):
            lhs = jnp.concatenate([(wide_hi[t:t + 1, :] * mask).astype(BF16),
                                   (wide_lo[t:t + 1, :] * mask).astype(BF16)], axis=0)
            acc = None
            for c in range(PEER_NSEL // PEER_CHUNK):
                tiles = []
                for m in range(PEER_CHUNK // 2):
                    base = t * PEER_NSEL + c * PEER_CHUNK + 2 * m
                    words = _pair_words(tab, idx_ref[0, base], idx_ref[0, base + 1])
                    tiles.append(pltpu.bitcast(words, BF16))
                k = 8 * PEER_CHUNK
                part = _dot(lhs[:, c * k:(c + 1) * k], jnp.concatenate(tiles, axis=0))
                acc = part if acc is None else acc + part
            ys = acc[0:8] + acc[8:16]
            outs.append(jnp.concatenate([ys[c:c + 1] for c in range(8)], axis=1))
        y = jnp.concatenate(outs, axis=0)
        o_ref[tok, :] = _rmsnorm(x1_ref[tok, :] + g2 * y, fg)

    _row_blocks(rows_hbm, (ibuf0, ibuf1), isem, body)


def _peer_out(rows, w, tab_v, x1, g2, fg, seq_len):
    t, d = x1.shape
    nb = t // PEER_TOK
    n = PEER_TOK * PEER_NSEL
    tok = pl.BlockSpec((2 * PEER_TOK, d), lambda i: (i, 0))
    return pl.pallas_call(
        _out_kernel,
        grid=(nb // 2,),
        in_specs=[pl.BlockSpec(memory_space=pl.ANY),
                  pl.BlockSpec(memory_space=pl.ANY),
                  pl.BlockSpec((2 * PEER_TOK, PEER_NSEL), lambda i: (i, 0)),
                  tok,
                  pl.BlockSpec((1, 1, d), lambda i: (i * 2 * PEER_TOK // seq_len, 0, 0)),
                  pl.BlockSpec((1, d), lambda i: (0, 0)),
                  pl.BlockSpec((PEER_NSEL, 8 * PEER_NSEL), lambda i: (0, 0)),
                  pl.BlockSpec((8, 8 * PEER_NSEL), lambda i: (0, 0))],
        out_specs=tok,
        out_shape=jax.ShapeDtypeStruct((t, d), F32),
        scratch_shapes=[pltpu.VMEM(tab_v.shape, jnp.uint32), pltpu.SemaphoreType.DMA((1,))] + _row_scratch(),
        compiler_params=pltpu.CompilerParams(dimension_semantics=("arbitrary",), vmem_limit_bytes=PEER_VMEM),
        name="peer_out",
    )(rows.reshape(nb, 1, n), tab_v, w, x1, g2, fg, _expand_matrix(), _chunk_row_mask())


def _block_diag(w):
    h, dh, _ = w.shape
    eye = jnp.eye(h, dtype=w.dtype)
    return (eye[:, None, :, None] * w[:, :, None, :]).reshape(h * dh, h * dh)


def kernel(x, c, ctx, c_ctx, w_mod, b_mod, norm1_g, w_in, conv_w, conv_b, lru_w_a, lru_b_a, lru_w_x,
           lru_b_x, lru_lambda, fourier_out_g, lru_out_g, w_out, norm2_g, peer_w_q, peer_sub_keys,
           peer_u, peer_v, final_norm_g):
    b, l, d = x.shape
    depth = w_mod.shape[0]
    assert depth == 1, "context stream update between layers is not implemented"
    assert d == D_MODEL and l % 1024 == 0 and b <= 7
    lyr = 0
    c_len = ctx.shape[1]

    cvec = jnp.zeros((8, d), F32).at[:b].set(c).at[b].set(c_ctx)
    mod = _mod(cvec, w_mod[lyr], b_mod[lyr])
    mod_x = mod[:b].reshape(b, N_MOD, 1, d)
    sh1, sc1, g1, sh2, sc2, g2 = [mod_x[:, k] for k in range(N_MOD)]
    mod_c = jnp.broadcast_to(mod[b].reshape(1, N_MOD, 1, d), (b, N_MOD, 1, d))

    n1g = norm1_g[lyr].reshape(1, d)
    w_in_b = w_in[lyr].astype(BF16)
    fx, ux, gx = _in_proj(x, sh1, sc1, n1g, w_in_b, tm=512)
    _, uc, _ = _in_proj(ctx, mod_c[:, 0], mod_c[:, 1], n1g, w_in_b, tm=c_len)

    wa = jnp.stack([_block_diag(lru_w_a[lyr, k]) for k in range(2)]).astype(BF16)
    wx = jnp.stack([_block_diag(lru_w_x[lyr, k]) for k in range(2)]).astype(BF16)
    lru_args = (conv_w[lyr], conv_b[lyr], wa, lru_b_a[lyr], wx, lru_b_x[lyr], lru_lambda[lyr])
    h_zero = jnp.zeros((b, 2, LRU_WIDTH), F32)
    _, _, fin = _lru(uc, *lru_args, h_zero, row_len=c_len, t=c_len)
    hf, hb, _ = _lru(ux, *lru_args, fin, row_len=GRID_W, t=256)

    fm = _fourier(fx, fourier_out_g[lyr])

    w_out_b = w_out[lyr].astype(BF16)
    wq_hi, wq_lo = _split_bf16(peer_w_q[lyr])
    x1, h2, q = _mix_out(hf, hb, gx, fm, x, g1, sh2, sc2, lru_out_g[lyr].reshape(1, LRU_WIDTH),
                         w_out_b[:FOURIER_WIDTH], w_out_b[FOURIER_WIDTH:], norm2_g[lyr].reshape(1, d),
                         wq_hi, wq_lo, tm=512)

    t = b * l
    keys = peer_sub_keys[lyr].reshape(2 * PEER_HEADS, PEER_NKEYS, PEER_KEY_HALF)
    k_hi, k_lo = _split_bf16(keys)
    rows, gate = _peer_topk(q.reshape(t, -1), k_hi, k_lo, tm=256)
    w = _peer_act(rows, _pack_table(peer_u[lyr]), h2.reshape(t, d), gate)
    out = _peer_out(rows, w, _pack_table(peer_v[lyr]), x1.reshape(t, d), g2,
                    final_norm_g.reshape(1, d), seq_len=l)
    return out.reshape(b, l, d)
```

```python
import functools

import jax
import jax.numpy as jnp
import numpy as np
from jax import lax
from jax.experimental import pallas as pl
from jax.experimental.pallas import tpu as pltpu

F32 = jnp.float32
BF16 = jnp.bfloat16

D_MODEL = 1024
GRID_W = 64
FOURIER_WIDTH = 512
FOURIER_HEADS = 8
FOURIER_HEAD_DIM = FOURIER_WIDTH // FOURIER_HEADS
LRU_WIDTH = 512
LRU_HEADS = 8
IN_WIDTH = FOURIER_WIDTH + 2 * LRU_WIDTH
CONV_W = 4
LRU_C = 8.0
PEER_HEADS = 8
PEER_NKEYS = 128
PEER_KEY_HALF = 128
PEER_TOPK = 16
PEER_NSEL = PEER_HEADS * PEER_TOPK
N_MOD = 6
EPS = 1e-6

HALF_ROWS = D_MODEL // 2 // 128
PACK_ROWS = 512
TAB_PAD = PACK_ROWS * HALF_ROWS
PEER_TOK = 16
PEER_VMEM = 52 * 1024 * 1024

VMEM_LIMIT = 48 * 1024 * 1024


def _params(*sem):
    return pltpu.CompilerParams(dimension_semantics=sem, vmem_limit_bytes=VMEM_LIMIT)


def _split_bf16(a):
    hi = a.astype(BF16)
    lo = (a - hi.astype(F32)).astype(BF16)
    return hi, lo


def _dot(a, b):
    return jnp.dot(a, b, preferred_element_type=F32)


def _dot3(a, b_hi, b_lo):
    a_hi, a_lo = _split_bf16(a)
    return _dot(a_hi, b_hi) + (_dot(a_hi, b_lo) + _dot(a_lo, b_hi))


def _rmsnorm(x, g):
    return x * lax.rsqrt(jnp.mean(x * x, axis=-1, keepdims=True) + EPS) * g


def _mod_kernel(c_ref, w_ref, b_ref, o_ref):
    c = c_ref[...]
    s = c * jax.nn.sigmoid(c)
    o_ref[...] = jnp.dot(s, w_ref[...], preferred_element_type=F32,
                         precision=lax.Precision.HIGHEST) + b_ref[...]


def _mod(cvec, w_mod, b_mod):
    n = w_mod.shape[1]
    bn = 768
    return pl.pallas_call(
        _mod_kernel,
        grid=(n // bn,),
        in_specs=[pl.BlockSpec((8, D_MODEL), lambda j: (0, 0)),
                  pl.BlockSpec((D_MODEL, bn), lambda j: (0, j)),
                  pl.BlockSpec((1, bn), lambda j: (0, j))],
        out_specs=pl.BlockSpec((8, bn), lambda j: (0, j)),
        out_shape=jax.ShapeDtypeStruct((8, n), F32),
        compiler_params=_params("arbitrary"),
        name="mod",
    )(cvec, w_mod, b_mod.reshape(1, n))


def _inproj_kernel(x_ref, sh_ref, sc_ref, g_ref, w_ref, f_ref, u_ref, gg_ref):
    h = _rmsnorm(x_ref[0], g_ref[...]) * (1.0 + sc_ref[0]) + sh_ref[0]
    o = _dot(h.astype(BF16), w_ref[...])
    f_ref[0] = o[:, :FOURIER_WIDTH]
    u_ref[0] = o[:, FOURIER_WIDTH:FOURIER_WIDTH + LRU_WIDTH]
    gg_ref[0] = o[:, FOURIER_WIDTH + LRU_WIDTH:]


def _in_proj(x, sh, sc, g, w_bf16, tm):
    b, l, _ = x.shape
    row = pl.BlockSpec((1, 1, D_MODEL), lambda i, j: (i, 0, 0))
    out = pl.BlockSpec((1, tm, FOURIER_WIDTH), lambda i, j: (i, j, 0))
    shp = jax.ShapeDtypeStruct((b, l, FOURIER_WIDTH), F32)
    return pl.pallas_call(
        _inproj_kernel,
        grid=(b, l // tm),
        in_specs=[pl.BlockSpec((1, tm, D_MODEL), lambda i, j: (i, j, 0)), row, row,
                  pl.BlockSpec((1, D_MODEL), lambda i, j: (0, 0)),
                  pl.BlockSpec((D_MODEL, IN_WIDTH), lambda i, j: (0, 0))],
        out_specs=[out, out, out],
        out_shape=[shp, shp, shp],
        compiler_params=_params("parallel", "arbitrary"),
        name="in_proj",
    )(x, sh, sc, g, w_bf16)


def _conv_rows(x, w, b, row_len):
    t = x.shape[0]
    col = lax.broadcasted_iota(jnp.int32, x.shape, 0) % row_len
    y = b + x * w[CONV_W // 2:CONV_W // 2 + 1]
    for k in range(CONV_W):
        off = k - CONV_W // 2
        if off == 0:
            continue
        shifted = pltpu.roll(x, (-off) % t, axis=0)
        ok = (col + off >= 0) & (col + off < row_len)
        y = y + jnp.where(ok, shifted, 0.0) * w[k:k + 1]
    return y


def _lru_terms(xc, wa, ba, wx, bx, sp):
    xb = xc.astype(BF16)
    r = jax.nn.sigmoid(_dot(xb, wa) + ba)
    i = jax.nn.sigmoid(_dot(xb, wx) + bx)
    log_a = (-LRU_C) * r * sp
    a = jnp.exp(log_a)
    th = jnp.tanh(log_a)
    one_minus_a2 = (-2.0) * th / (1.0 - th)
    return a, jnp.sqrt(one_minus_a2) * (i * xc)


def _scan_block(a, bv, h0, reverse):
    t = a.shape[0]
    row = lax.broadcasted_iota(jnp.int32, a.shape, 0) % 8
    for d in (1, 2, 4):
        if reverse:
            ok = row < 8 - d
            shift = t - d
        else:
            ok = row >= d
            shift = d
        a_s = jnp.where(ok, pltpu.roll(a, shift, axis=0), 1.0)
        b_s = jnp.where(ok, pltpu.roll(bv, shift, axis=0), 0.0)
        bv = a * b_s + bv
        a = a * a_s
    ngroups = t // 8
    hs = [None] * ngroups
    carry = h0
    for g in (reversed(range(ngroups)) if reverse else range(ngroups)):
        hg = a[8 * g:8 * g + 8] * carry + bv[8 * g:8 * g + 8]
        hs[g] = hg
        carry = hg[0:1] if reverse else hg[7:8]
    return jnp.concatenate(hs, axis=0)


def _lru_kernel(uf_ref, ub_ref, cw_ref, cb_ref, wa_ref, ba_ref, wx_ref, bx_ref, lam_ref, h0_ref,
                hf_ref, hb_ref, fin_ref, carry, *, row_len):
    j = pl.program_id(1)

    @pl.when(j == 0)
    def _():
        carry[...] = h0_ref[0]

    cw = cw_ref[...]
    cb = cb_ref[...]
    t = uf_ref.shape[1]
    lam = lam_ref[...]
    sp = jax.nn.softplus(-lam)
    for d, (u_ref, o_ref) in enumerate(((uf_ref, hf_ref), (ub_ref, hb_ref))):
        xc = _conv_rows(u_ref[0], cw, cb, row_len)
        a, bv = _lru_terms(xc, wa_ref[d], ba_ref[d:d + 1], wx_ref[d], bx_ref[d:d + 1], sp[d:d + 1])
        h = _scan_block(a, bv, carry[d:d + 1], reverse=(d == 1))
        o_ref[0] = h
        last = h[0:1] if d == 1 else h[t - 1:t]
        carry[d:d + 1] = last
    fin_ref[0] = carry[...]


def _lru(u, conv_w, conv_b, wa_bd, ba, wx_bd, bx, lam, h0, row_len, t):
    b, l, c = u.shape
    nb = l // t
    kern = functools.partial(_lru_kernel, row_len=row_len)
    full = lambda shape: pl.BlockSpec(shape, lambda i, j: (0,) * len(shape))
    blk_f = pl.BlockSpec((1, t, c), lambda i, j: (i, j, 0))
    blk_b = pl.BlockSpec((1, t, c), lambda i, j: (i, nb - 1 - j, 0))
    st = pl.BlockSpec((1, 2, c), lambda i, j: (i, 0, 0))
    return pl.pallas_call(
        kern,
        grid=(b, nb),
        in_specs=[blk_f, blk_b, full((CONV_W, c)), full((1, c)), full((2, c, c)), full((2, c)),
                  full((2, c, c)), full((2, c)), full((2, c)), st],
        out_specs=[blk_f, blk_b, st],
        out_shape=[jax.ShapeDtypeStruct((b, l, c), F32), jax.ShapeDtypeStruct((b, l, c), F32),
                   jax.ShapeDtypeStruct((b, 2, c), F32)],
        scratch_shapes=[pltpu.VMEM((2, c), F32)],
        compiler_params=_params("parallel", "arbitrary"),
        name="lru",
    )(u, u, conv_w, conv_b.reshape(1, c), wa_bd, ba, wx_bd, bx, lam, h0)


def _dft_rows_kernel(f_ref, c_ref, s_ref, ar_ref, ai_ref):
    x = f_ref[0].astype(BF16)
    ar_ref[0] = _dot(c_ref[...], x)
    ai_ref[0] = -_dot(s_ref[...], x)


def _dft_cols_kernel(ar_ref, ai_ref, tr_ref, ti_ref, c_ref, s_ref, bc_ref, bs_ref, g_ref, o_ref):
    c = c_ref[...]
    s = s_ref[...]
    w = g_ref.shape[1]
    for k in range(ar_ref.shape[1]):
        ar = ar_ref[0, k]
        ai = ai_ref[0, k]
        tr = tr_ref[k]
        ti = ti_ref[k]
        br = (ar * tr - ai * ti).astype(BF16)
        bi = (ar * ti + ai * tr).astype(BF16)
        zr = _dot(c, br) + _dot(s, bi)
        zi = _dot(c, bi) - _dot(s, br)
        y = _dot(zr.astype(BF16), bc_ref[...]) + _dot(zi.astype(BF16), bs_ref[...])
        o_ref[0, :, k * w:(k + 1) * w] = _rmsnorm(y, g_ref[...]).astype(o_ref.dtype)


DFT_K1 = 4


def _dft_mats(n):
    k = np.arange(n)
    ang = 2.0 * np.pi * ((k[:, None] * k[None, :]) % n) / n
    return np.cos(ang), np.sin(ang)


def _fourier(f, g):
    b, l, w = f.shape
    n2 = 128
    n1 = l // n2
    scale = 1.0 / np.sqrt(float(l) * FOURIER_HEAD_DIM)
    c1, s1 = _dft_mats(n1)
    c2, s2 = _dft_mats(n2)
    cc, sc = _dft_mats(FOURIER_HEAD_DIM)
    eye = np.eye(FOURIER_HEADS)
    bc = np.kron(eye, cc) * scale
    bs = np.kron(eye, sc) * scale
    ang = 2.0 * np.pi * ((np.arange(n1)[:, None] * np.arange(n2)[None, :]) % l) / l
    tr = jnp.asarray(np.cos(ang)[:, :, None], F32)
    ti = jnp.asarray(-np.sin(ang)[:, :, None], F32)
    as_bf = lambda m: jnp.asarray(m, BF16)

    cb = min(8192, n2 * w)
    f2 = f.reshape(b, n1, n2 * w)
    blk = pl.BlockSpec((1, n1, cb), lambda i, j: (i, 0, j))
    mat1 = pl.BlockSpec((n1, n1), lambda i, j: (0, 0))
    shp = jax.ShapeDtypeStruct((b, n1, n2 * w), F32)
    ar, ai = pl.pallas_call(
        _dft_rows_kernel,
        grid=(b, n2 * w // cb),
        in_specs=[blk, mat1, mat1],
        out_specs=[blk, blk],
        out_shape=[shp, shp],
        compiler_params=_params("parallel", "arbitrary"),
        name="dft_rows",
    )(f2, as_bf(c1), as_bf(s1))

    ar = ar.reshape(b, n1, n2, w)
    ai = ai.reshape(b, n1, n2, w)
    kb = DFT_K1
    a_blk = pl.BlockSpec((1, kb, n2, w), lambda i, j: (i, j, 0, 0))
    t_blk = pl.BlockSpec((kb, n2, 1), lambda i, j: (j, 0, 0))
    mat2 = pl.BlockSpec((n2, n2), lambda i, j: (0, 0))
    matw = pl.BlockSpec((w, w), lambda i, j: (0, 0))
    out = pl.pallas_call(
        _dft_cols_kernel,
        grid=(b, n1 // kb),
        in_specs=[a_blk, a_blk, t_blk, t_blk, mat2, mat2, matw, matw,
                  pl.BlockSpec((1, w), lambda i, j: (0, 0))],
        out_specs=pl.BlockSpec((1, n2, kb * w), lambda i, j: (i, 0, j)),
        out_shape=jax.ShapeDtypeStruct((b, n2, n1 * w), BF16),
        compiler_params=_params("parallel", "arbitrary"),
        name="dft_cols",
    )(ar, ai, tr, ti, as_bf(c2), as_bf(s2), as_bf(bc), as_bf(bs), g.reshape(1, w))
    return out.reshape(b, l, w)


def _mixout_kernel(hf_ref, hb_ref, gg_ref, fm_ref, x_ref, g1_ref, sh2_ref, sc2_ref, lg_ref,
                   wo_f_ref, wo_r_ref, n2g_ref, wq_hi_ref, wq_lo_ref, x1_ref, h2_ref, q_ref):
    rx = (hf_ref[0] + hb_ref[0]) * jax.nn.gelu(gg_ref[0])
    rxn = _rmsnorm(rx, lg_ref[...])
    mx = _dot(fm_ref[0], wo_f_ref[...]) + _dot(rxn.astype(BF16), wo_r_ref[...])
    x1 = x_ref[0] + g1_ref[0] * mx
    x1_ref[0] = x1
    h2 = _rmsnorm(x1, n2g_ref[...]) * (1.0 + sc2_ref[0]) + sh2_ref[0]
    h2_ref[0] = h2
    q_ref[0] = _dot3(h2, wq_hi_ref[...], wq_lo_ref[...])


def _mix_out(hf, hb, gg, fm, x, g1, sh2, sc2, lru_g, wo_f, wo_r, n2g, wq_hi, wq_lo, tm):
    b, l, d = x.shape
    nq = wq_hi.shape[1]
    half = pl.BlockSpec((1, tm, LRU_WIDTH), lambda i, j: (i, j, 0))
    tok = pl.BlockSpec((1, tm, d), lambda i, j: (i, j, 0))
    row = pl.BlockSpec((1, 1, d), lambda i, j: (i, 0, 0))
    full = lambda shape: pl.BlockSpec(shape, lambda i, j: (0,) * len(shape))
    return pl.pallas_call(
        _mixout_kernel,
        grid=(b, l // tm),
        in_specs=[half, half, half, half, tok, row, row, row, full((1, LRU_WIDTH)),
                  full((FOURIER_WIDTH, d)), full((LRU_WIDTH, d)), full((1, d)),
                  full((d, nq)), full((d, nq))],
        out_specs=[tok, tok, pl.BlockSpec((1, tm, nq), lambda i, j: (i, j, 0))],
        out_shape=[jax.ShapeDtypeStruct((b, l, d), F32), jax.ShapeDtypeStruct((b, l, d), F32),
                   jax.ShapeDtypeStruct((b, l, nq), F32)],
        compiler_params=_params("parallel", "arbitrary"),
        name="mix_out",
    )(hf, hb, gg, fm, x, g1, sh2, sc2, lru_g, wo_f, wo_r, n2g, wq_hi, wq_lo)


def _top16(s, payload=None):
    n = s.shape[0]
    iota = lax.broadcasted_iota(jnp.int32, s.shape, 0).astype(F32)
    vals, picks = [], []
    for _ in range(PEER_TOPK):
        m = jnp.max(s, axis=0, keepdims=True)
        pos = jnp.min(jnp.where(s == m, iota, float(n)), axis=0, keepdims=True)
        hit = iota == pos
        vals.append(m)
        if payload is None:
            picks.append(pos)
        else:
            picks.append(jnp.max(jnp.where(hit, payload, -1.0), axis=0, keepdims=True))
        s = jnp.where(hit, -jnp.inf, s)
    return jnp.concatenate(vals, axis=0), jnp.concatenate(picks, axis=0)


def _candidate_grid(s1, i1, s2, i2):
    k = PEER_TOPK
    row8 = lax.broadcasted_iota(jnp.int32, (8, s1.shape[1]), 0)
    vals = [s1[0:1] + s2, s1[1:2] + s2[0:8]]
    idxs = [i1[0:1] * PEER_NKEYS + i2, i1[1:2] * PEER_NKEYS + i2[0:8]]
    for i in range(2, 8):
        ok = row8 < k // (i + 1)
        vals.append(jnp.where(ok, s1[i:i + 1] + s2[0:8], -jnp.inf))
        idxs.append(i1[i:i + 1] * PEER_NKEYS + i2[0:8])
    vals.append(s1[8:16] + s2[0:1])
    idxs.append(i1[8:16] * PEER_NKEYS + i2[0:1])
    return jnp.concatenate(vals, axis=0), jnp.concatenate(idxs, axis=0)


def _topk_kernel(q_ref, k_hi_ref, k_lo_ref, idx_ref, g_ref):
    nt = (((1,), (1,)), ((), ()))
    idx_rows, gate_rows = [], []
    for h in range(PEER_HEADS):
        tops = []
        for p in range(2):
            hp = 2 * h + p
            qs = q_ref[:, hp * PEER_KEY_HALF:(hp + 1) * PEER_KEY_HALF]
            q_hi, q_lo = _split_bf16(qs)
            k_hi = k_hi_ref[hp]
            k_lo = k_lo_ref[hp]
            dg = lambda a, b: lax.dot_general(a, b, nt, preferred_element_type=F32)
            s = dg(k_hi, q_hi) + (dg(k_hi, q_lo) + dg(k_lo, q_hi))
            tops.append(_top16(s))
        (s1, i1), (s2, i2) = tops
        cand, cidx = _candidate_grid(s1, i1, s2, i2)
        sc, idx = _top16(cand, cidx)
        e = jnp.exp(sc - sc[0:1])
        g = e / jnp.sum(e, axis=0, keepdims=True)
        idx_rows.append(idx)
        gate_rows.append(g)
    rows = jnp.concatenate(idx_rows, axis=0) * float(HALF_ROWS) + float(TAB_PAD)
    idx_ref[...] = rows.T.astype(jnp.int32)
    g_ref[...] = jnp.concatenate(gate_rows, axis=0).T


def _peer_topk(q, k_hi, k_lo, tm):
    t, nq = q.shape
    nsel = PEER_NSEL
    full = pl.BlockSpec(k_hi.shape, lambda i: (0, 0, 0))
    out = pl.BlockSpec((tm, nsel), lambda i: (i, 0))
    return pl.pallas_call(
        _topk_kernel,
        grid=(t // tm,),
        in_specs=[pl.BlockSpec((tm, nq), lambda i: (i, 0)), full, full],
        out_specs=[out, out],
        out_shape=[jax.ShapeDtypeStruct((t, nsel), jnp.int32), jax.ShapeDtypeStruct((t, nsel), F32)],
        compiler_params=_params("parallel"),
        name="peer_topk",
    )(q, k_hi, k_lo)


def _pack_kernel(w_ref, o_ref):
    i = pl.program_id(0)
    pad = (i == 0) | (i == pl.num_programs(0) - 1)

    @pl.when(pad)
    def _():
        o_ref[...] = jnp.zeros(o_ref.shape, o_ref.dtype)

    @pl.when(jnp.logical_not(pad))
    def _():
        bits = pltpu.bitcast(w_ref[...], jnp.uint32)
        bf = (bits + jnp.uint32(0x7FFF) + ((bits >> 16) & jnp.uint32(1))) >> 16
        half = bf.shape[1] // 2
        words = bf[:, :half] | (bf[:, half:] << 16)
        for c in range(HALF_ROWS):
            o_ref[pl.ds(c, PACK_ROWS, stride=HALF_ROWS), :] = words[:, c * 128:(c + 1) * 128]


def _pack_table(w):
    e, d = w.shape
    nblk = e // PACK_ROWS
    return pl.pallas_call(
        _pack_kernel,
        grid=(nblk + 2,),
        in_specs=[pl.BlockSpec((PACK_ROWS, d), lambda i: (jnp.clip(i - 1, 0, nblk - 1), 0))],
        out_specs=pl.BlockSpec((TAB_PAD, 128), lambda i: (i, 0)),
        out_shape=jax.ShapeDtypeStruct((TAB_PAD * (nblk + 2), 128), jnp.uint32),
        compiler_params=_params("parallel"),
        name="pack_table",
    )(w)


def _halves(words):
    lo = pltpu.bitcast(words << 16, F32)
    hi = pltpu.bitcast(words & jnp.uint32(0xFFFF0000), F32)
    return lo, hi


def _slab_pair(row_ref, t):
    def tile(base):
        rows = [row_ref[t:t + 1, base + c * 128:base + (c + 1) * 128] for c in range(HALF_ROWS)]
        return jnp.concatenate(rows + rows, axis=0)
    return tile(0), tile(D_MODEL // 2)


def _pair_words(tab, row_a, row_b):
    sub = lax.broadcasted_iota(jnp.int32, (8, 128), 0)
    wa = tab[pl.ds(pl.multiple_of(row_a, HALF_ROWS), 8), :]
    wb = tab[pl.ds(pl.multiple_of(row_b - HALF_ROWS, HALF_ROWS), 8), :]
    return jnp.where(sub < HALF_ROWS, wa, wb)


PEER_CHUNK = 32


def _pair_sum_matrix():
    g = np.zeros((PEER_CHUNK, 4 * PEER_CHUNK), np.float32)
    for p in range(PEER_CHUNK):
        g[p, 4 * p:4 * p + 4] = 1.0
    return jnp.asarray(g, BF16)


def _expand_matrix():
    r = np.zeros((PEER_NSEL, 8 * PEER_NSEL), np.float32)
    for p in range(PEER_NSEL):
        r[p, 8 * p:8 * p + 8] = 1.0
    return jnp.asarray(r, BF16)


def _chunk_row_mask():
    q = np.arange(8 * PEER_NSEL)
    s, h = (q % 16) // 2, q % 2
    target = (s % 4) + 4 * h
    return jnp.asarray((target[None, :] == np.arange(8)[:, None]).astype(np.float32), F32)


def _load_table(tab_hbm, tab, sem):
    cp = pltpu.make_async_copy(tab_hbm, tab, sem.at[0])
    cp.start()
    cp.wait()


def _row_blocks(rows_hbm, ibufs, isem, body):
    j = pl.program_id(0)
    nblk = 2 * pl.num_programs(0)

    def rows_copy(blk, s):
        return pltpu.make_async_copy(rows_hbm.at[blk], ibufs[s], isem.at[s])

    @pl.when(j == 0)
    def _():
        rows_copy(0, 0).start()

    for s in range(2):
        blk = 2 * j + s

        @pl.when(blk + 1 < nblk)
        def _():
            rows_copy(blk + 1, 1 - s).start()

        rows_copy(blk, s).wait()
        body(s, ibufs[s])


def _row_scratch():
    n = PEER_TOK * PEER_NSEL
    return [pltpu.SMEM((1, n), jnp.int32), pltpu.SMEM((1, n), jnp.int32), pltpu.SemaphoreType.DMA((2,))]


def _act_kernel(rows_hbm, tab_hbm, h2_ref, gate_ref, gs_ref, w_ref, tab, tsem, ibuf0, ibuf1, isem):
    @pl.when(pl.program_id(0) == 0)
    def _():
        _load_table(tab_hbm, tab, tsem)

    gsum = gs_ref[...]

    def body(s, idx_ref):
        cols = []
        for t in range(PEER_TOK):
            xa, xb = _slab_pair(h2_ref, s * PEER_TOK + t)
            sums = []
            for c in range(PEER_NSEL // PEER_CHUNK):
                prods = []
                for m in range(PEER_CHUNK // 2):
                    base = t * PEER_NSEL + c * PEER_CHUNK + 2 * m
                    lo, hi = _halves(_pair_words(tab, idx_ref[0, base], idx_ref[0, base + 1]))
                    prods.append(lo * xa + hi * xb)
                stack = jnp.concatenate(prods, axis=0).astype(BF16)
                sums.append(_dot(gsum, stack))
            cols.append(jnp.sum(jnp.concatenate(sums, axis=0), axis=-1, keepdims=True))
        act = jnp.concatenate(cols, axis=1).T
        tok = slice(s * PEER_TOK, (s + 1) * PEER_TOK)
        w_ref[tok, :] = gate_ref[tok, :] * jax.nn.gelu(act)

    _row_blocks(rows_hbm, (ibuf0, ibuf1), isem, body)


def _peer_act(rows, tab_u, h2, gate):
    t, d = h2.shape
    nb = t // PEER_TOK
    n = PEER_TOK * PEER_NSEL
    sel = pl.BlockSpec((2 * PEER_TOK, PEER_NSEL), lambda i: (i, 0))
    return pl.pallas_call(
        _act_kernel,
        grid=(nb // 2,),
        in_specs=[pl.BlockSpec(memory_space=pl.ANY),
                  pl.BlockSpec(memory_space=pl.ANY),
                  pl.BlockSpec((2 * PEER_TOK, d), lambda i: (i, 0)),
                  sel,
                  pl.BlockSpec((PEER_CHUNK, 4 * PEER_CHUNK), lambda i: (0, 0))],
        out_specs=sel,
        out_shape=jax.ShapeDtypeStruct((t, PEER_NSEL), F32),
        scratch_shapes=[pltpu.VMEM(tab_u.shape, jnp.uint32), pltpu.SemaphoreType.DMA((1,))] + _row_scratch(),
        compiler_params=pltpu.CompilerParams(dimension_semantics=("arbitrary",), vmem_limit_bytes=PEER_VMEM),
        name="peer_act",
    )(rows.reshape(nb, 1, n), tab_u, h2, gate, _pair_sum_matrix())


def _out_kernel(rows_hbm, tab_hbm, w_ref, x1_ref, g2_ref, fg_ref, ex_ref, mk_ref, o_ref, tab, tsem,
                ibuf0, ibuf1, isem):
    @pl.when(pl.program_id(0) == 0)
    def _():
        _load_table(tab_hbm, tab, tsem)

    g2 = g2_ref[0]
    fg = fg_ref[...]

    expand = ex_ref[...]
    mask = mk_ref[...]

    def body(s, idx_ref):
        tok = slice(s * PEER_TOK, (s + 1) * PEER_TOK)
        w_hi, w_lo = _split_bf16(w_ref[tok, :])
        wide_hi = _dot(w_hi, expand)
        wide_lo = _dot(w_lo, expand)
        outs = []
        for t in range(PEER_TOK):
            lhs = jnp.concatenate([(wide_hi[t:t + 1, :] * mask).astype(BF16),
                                   (wide_lo[t:t + 1, :] * mask).astype(BF16)], axis=0)
            acc = None
            for c in range(PEER_NSEL // PEER_CHUNK):
                tiles = []
                for m in range(PEER_CHUNK // 2):
                    base = t * PEER_NSEL + c * PEER_CHUNK + 2 * m
                    words = _pair_words(tab, idx_ref[0, base], idx_ref[0, base + 1])
                    tiles.append(pltpu.bitcast(words, BF16))
                k = 8 * PEER_CHUNK
                part = _dot(lhs[:, c * k:(c + 1) * k], jnp.concatenate(tiles, axis=0))
                acc = part if acc is None else acc + part
            ys = acc[0:8] + acc[8:16]
            outs.append(jnp.concatenate([ys[c:c + 1] for c in range(8)], axis=1))
        y = jnp.concatenate(outs, axis=0)
        o_ref[tok, :] = _rmsnorm(x1_ref[tok, :] + g2 * y, fg)

    _row_blocks(rows_hbm, (ibuf0, ibuf1), isem, body)


def _peer_out(rows, w, tab_v, x1, g2, fg, seq_len):
    t, d = x1.shape
    nb = t // PEER_TOK
    n = PEER_TOK * PEER_NSEL
    tok = pl.BlockSpec((2 * PEER_TOK, d), lambda i: (i, 0))
    return pl.pallas_call(
        _out_kernel,
        grid=(nb // 2,),
        in_specs=[pl.BlockSpec(memory_space=pl.ANY),
                  pl.BlockSpec(memory_space=pl.ANY),
                  pl.BlockSpec((2 * PEER_TOK, PEER_NSEL), lambda i: (i, 0)),
                  tok,
                  pl.BlockSpec((1, 1, d), lambda i: (i * 2 * PEER_TOK // seq_len, 0, 0)),
                  pl.BlockSpec((1, d), lambda i: (0, 0)),
                  pl.BlockSpec((PEER_NSEL, 8 * PEER_NSEL), lambda i: (0, 0)),
                  pl.BlockSpec((8, 8 * PEER_NSEL), lambda i: (0, 0))],
        out_specs=tok,
        out_shape=jax.ShapeDtypeStruct((t, d), F32),
        scratch_shapes=[pltpu.VMEM(tab_v.shape, jnp.uint32), pltpu.SemaphoreType.DMA((1,))] + _row_scratch(),
        compiler_params=pltpu.CompilerParams(dimension_semantics=("arbitrary",), vmem_limit_bytes=PEER_VMEM),
        name="peer_out",
    )(rows.reshape(nb, 1, n), tab_v, w, x1, g2, fg, _expand_matrix(), _chunk_row_mask())


def _block_diag(w):
    h, dh, _ = w.shape
    eye = jnp.eye(h, dtype=w.dtype)
    return (eye[:, None, :, None] * w[:, :, None, :]).reshape(h * dh, h * dh)


def kernel(x, c, ctx, c_ctx, w_mod, b_mod, norm1_g, w_in, conv_w, conv_b, lru_w_a, lru_b_a, lru_w_x,
           lru_b_x, lru_lambda, fourier_out_g, lru_out_g, w_out, norm2_g, peer_w_q, peer_sub_keys,
           peer_u, peer_v, final_norm_g):
    b, l, d = x.shape
    depth = w_mod.shape[0]
    assert depth == 1, "context stream update between layers is not implemented"
    assert d == D_MODEL and l % 1024 == 0 and b <= 7
    lyr = 0
    c_len = ctx.shape[1]

    cvec = jnp.zeros((8, d), F32).at[:b].set(c).at[b].set(c_ctx)
    mod = _mod(cvec, w_mod[lyr], b_mod[lyr])
    mod_x = mod[:b].reshape(b, N_MOD, 1, d)
    sh1, sc1, g1, sh2, sc2, g2 = [mod_x[:, k] for k in range(N_MOD)]
    mod_c = jnp.broadcast_to(mod[b].reshape(1, N_MOD, 1, d), (b, N_MOD, 1, d))

    n1g = norm1_g[lyr].reshape(1, d)
    w_in_b = w_in[lyr].astype(BF16)
    fx, ux, gx = _in_proj(x, sh1, sc1, n1g, w_in_b, tm=512)
    _, uc, _ = _in_proj(ctx, mod_c[:, 0], mod_c[:, 1], n1g, w_in_b, tm=c_len)

    wa = jnp.stack([_block_diag(lru_w_a[lyr, k]) for k in range(2)]).astype(BF16)
    wx = jnp.stack([_block_diag(lru_w_x[lyr, k]) for k in range(2)]).astype(BF16)
    lru_args = (conv_w[lyr], conv_b[lyr], wa, lru_b_a[lyr], wx, lru_b_x[lyr], lru_lambda[lyr])
    h_zero = jnp.zeros((b, 2, LRU_WIDTH), F32)
    _, _, fin = _lru(uc, *lru_args, h_zero, row_len=c_len, t=c_len)
    hf, hb, _ = _lru(ux, *lru_args, fin, row_len=GRID_W, t=256)

    fm = _fourier(fx, fourier_out_g[lyr])

    w_out_b = w_out[lyr].astype(BF16)
    wq_hi, wq_lo = _split_bf16(peer_w_q[lyr])
    x1, h2, q = _mix_out(hf, hb, gx, fm, x, g1, sh2, sc2, lru_out_g[lyr].reshape(1, LRU_WIDTH),
                         w_out_b[:FOURIER_WIDTH], w_out_b[FOURIER_WIDTH:], norm2_g[lyr].reshape(1, d),
                         wq_hi, wq_lo, tm=512)

    t = b * l
    keys = peer_sub_keys[lyr].reshape(2 * PEER_HEADS, PEER_NKEYS, PEER_KEY_HALF)
    k_hi, k_lo = _split_bf16(keys)
    rows, gate = _peer_topk(q.reshape(t, -1), k_hi, k_lo, tm=256)
    w = _peer_act(rows, _pack_table(peer_u[lyr]), h2.reshape(t, d), gate)
    out = _peer_out(rows, w, _pack_table(peer_v[lyr]), x1.reshape(t, d), g2,
                    final_norm_g.reshape(1, d), seq_len=l)
    return out.reshape(b, l, d)
```
